```python
import jax
import jax.numpy as jnp
from jax import lax
import numpy as np

D_MODEL = 2048
BATCH = 32
SEQ = 256
DEPTH = 1
DEC_BATCH = 8
DEC_SEQ = 4096
PAST_LEN = 256

GRID_W = 64
D_RNN = D_MODEL
RNN_BLOCKS = 8
RNN_BLOCK = D_RNN // RNN_BLOCKS
CONV_W = 4
RG_C = 8.0
N_HEADS = 16
HEAD_DIM = 64
D_NA = N_HEADS * HEAD_DIM
WIN_ROWS = 8
WIN_COLS = 16
N_EXPERTS = 32
TOP_K = 4
D_FF = D_MODEL
SWIGLU_LIMIT = 7.0
SWIGLU_ALPHA = 1.702
MOE_BLOCK = 256
Q_BLOCK = 128
D_IN = D_RNN + 3 * D_NA + 2 * D_MODEL
ALPHA = (2 * DEPTH) ** 0.25
BETA = (8 * DEPTH) ** -0.25
LN_EPS = 1e-5
ATTN_SCALE = HEAD_DIM ** -0.5
NEG_INF = -1e30
F32 = jnp.float32

kernel_name = 'hybrid_rglru_natten_moe_diffusion_step'


def _layernorm(x, g, b):
    xf = x.astype(F32)
    mu = jnp.mean(xf, axis=-1, keepdims=True)
    var = jnp.mean(jnp.square(xf - mu), axis=-1, keepdims=True)
    return ((xf - mu) * lax.rsqrt(var + LN_EPS) * g.astype(F32) + b.astype(F32)).astype(x.dtype)


def _adaln(cond, w_ada, b_ada):
    mod = jax.nn.silu(cond) @ w_ada + b_ada
    return [m[:, None, :] for m in jnp.split(mod, 6, axis=-1)]


def _split_proj(u, w_in, b_in):
    proj = u @ w_in + b_in
    cuts = [D_RNN, D_RNN + D_NA, D_RNN + 2 * D_NA, D_RNN + 3 * D_NA, D_RNN + 3 * D_NA + D_MODEL]
    return jnp.split(proj, cuts, axis=-1)


def _heads(t):
    b, n, _ = t.shape
    return t.reshape(b, n, N_HEADS, HEAD_DIM).transpose(0, 2, 1, 3)


def _dwconv_centred(x, w, b):
    t = x.shape[1]
    left = CONV_W // 2
    xp = jnp.pad(x, ((0, 0), (left, CONV_W - 1 - left), (0, 0)))
    y = b + xp[:, 0:t] * w[0]
    for j in range(1, CONV_W):
        y = y + xp[:, j:j + t] * w[j]
    return y


def _linear_scan(a, b, h0, reverse):
    def step(h, ab):
        a_t, b_t = ab
        h = a_t * h + b_t
        return h, h
    h_last, hs = lax.scan(step, h0, (jnp.swapaxes(a, 0, 1), jnp.swapaxes(b, 0, 1)), reverse=reverse)
    return jnp.swapaxes(hs, 0, 1), h_last


def _rglru_bidir(xc, lw, h0_f, h0_b, reset):
    bsz, t, _ = xc.shape
    xf = xc.astype(F32)
    xb = xf.reshape(bsz, t, RNN_BLOCKS, RNN_BLOCK)
    r_gate = jax.nn.sigmoid(jnp.einsum('btnj,dnjk->dbtnk', xb, lw['rg_wa'].astype(F32)).reshape(2, bsz, t, D_RNN)
                            + lw['rg_ba'].astype(F32)[:, None, None, :])
    i_gate = jax.nn.sigmoid(jnp.einsum('btnj,dnjk->dbtnk', xb, lw['rg_wx'].astype(F32)).reshape(2, bsz, t, D_RNN)
                            + lw['rg_bx'].astype(F32)[:, None, None, :])
    log_a = -RG_C * r_gate * jax.nn.softplus(-lw['rg_lam'].astype(F32))[:, None, None, :]
    a = jnp.exp(log_a)
    mult = jnp.sqrt(-jnp.expm1(2.0 * log_a))
    if reset:
        pos = jnp.arange(t)
        first = jnp.stack([pos == 0, pos == t - 1])
        mult = jnp.where(first[:, None, :, None], 1.0, mult)
    b_in = mult * i_gate * xf[None]
    y_f, h_f = _linear_scan(a[0], b_in[0], h0_f.astype(F32), False)
    y_b, h_b = _linear_scan(a[1], b_in[1], h0_b.astype(F32), True)
    return y_f + y_b, h_f, h_b


def _context_attention(q, k, v):
    bsz, h, s, hd = q.shape
    nq = s // Q_BLOCK
    qb = q.reshape(bsz, h, nq, Q_BLOCK, hd).transpose(2, 0, 1, 3, 4)

    def block(qi):
        sc = jnp.einsum('bhqd,bhkd->bhqk', qi, k).astype(F32) * ATTN_SCALE
        p = jax.nn.softmax(sc, axis=-1).astype(v.dtype)
        return jnp.einsum('bhqk,bhkd->bhqd', p, v)

    o = lax.map(block, qb)
    return o.transpose(1, 0, 3, 2, 4).reshape(bsz, s, h * hd)


def _neighbourhood_attention(q, k, v, ck, cv, rpb):
    bsz, h, t, hd = q.shape
    rows = t // GRID_W
    wr = min(WIN_ROWS, rows)
    qg = q.reshape(bsz, h, rows, GRID_W, hd)
    kg = k.reshape(bsz, h, rows, GRID_W, hd)
    vg = v.reshape(bsz, h, rows, GRID_W, hd)
    rstart = jnp.clip(jnp.arange(rows) - wr // 2, 0, rows - wr)
    cols = jnp.arange(GRID_W)
    cstart = jnp.clip(cols - WIN_COLS // 2, 0, GRID_W - WIN_COLS)
    col_mask = (cols[None, :] >= cstart[:, None]) & (cols[None, :] < cstart[:, None] + WIN_COLS)
    col_idx = jnp.clip(cols[None, :] - cols[:, None] + WIN_COLS - 1, 0, 2 * WIN_COLS - 2)
    n_lat = wr * GRID_W

    def row(args):
        r, q_r = args
        rs = rstart[r]
        k_s = lax.dynamic_slice_in_dim(kg, rs, wr, axis=2)
        v_s = lax.dynamic_slice_in_dim(vg, rs, wr, axis=2)
        dr = rs + jnp.arange(wr) - r + WIN_ROWS - 1
        bias = rpb[:, dr[:, None, None], col_idx[None, :, :]].astype(F32)
        s_lat = jnp.einsum('bhqd,bhwkd->bhqwk', q_r, k_s).astype(F32) * ATTN_SCALE + jnp.transpose(bias, (0, 2, 1, 3))[None]
        s_lat = jnp.where(col_mask[:, None, :], s_lat, NEG_INF)
        s_ctx = jnp.einsum('bhqd,bhpd->bhqp', q_r, ck).astype(F32) * ATTN_SCALE
        sc = jnp.concatenate([s_lat.reshape(bsz, h, GRID_W, n_lat), s_ctx], axis=-1)
        p = jax.nn.softmax(sc, axis=-1).astype(v.dtype)
        p_lat = p[..., :n_lat].reshape(bsz, h, GRID_W, wr, GRID_W)
        return (jnp.einsum('bhqwk,bhwkd->bhqd', p_lat, v_s)
                + jnp.einsum('bhqp,bhpd->bhqd', p[..., n_lat:], cv))

    o = lax.map(row, (jnp.arange(rows), qg.transpose(2, 0, 1, 3, 4)))
    return o.transpose(1, 0, 3, 2, 4).reshape(bsz, t, h * hd)


def _merge(y_rnn, y_na, g_a, g_b, lw):
    y = jax.nn.sigmoid(g_a) * (y_rnn @ lw['w_rnn_out']) + jax.nn.sigmoid(g_b) * (y_na @ lw['w_na_out'])
    return y @ lw['w_o'] + lw['b_o']


def _mixer_context(u, lw):
    xr, q, k, v, g_a, g_b = _split_proj(u, lw['w_in'], lw['b_in'])
    xc = _dwconv_centred(xr, lw['conv_w'], lw['conv_b'])
    h0 = jnp.zeros((u.shape[0], D_RNN), F32)
    y_rnn, h_f, h_b = _rglru_bidir(xc, lw, h0, h0, True)
    kh, vh = _heads(k), _heads(v)
    y_na = _context_attention(_heads(q), kh, vh)
    return _merge(y_rnn.astype(u.dtype), y_na, g_a, g_b, lw), kh, vh, h_f, h_b


def _mixer_latent(u, ck, cv, h0_f, h0_b, lw):
    xr, q, k, v, g_a, g_b = _split_proj(u, lw['w_in'], lw['b_in'])
    xc = _dwconv_centred(xr, lw['conv_w'], lw['conv_b'])
    y_rnn, _, _ = _rglru_bidir(xc, lw, h0_f, h0_b, False)
    y_na = _neighbourhood_attention(_heads(q), _heads(k), _heads(v), ck, cv, lw['rpb'])
    return _merge(y_rnn.astype(u.dtype), y_na, g_a, g_b, lw)


def _moe(x, w_router, b_router, w_gu, b_gu, w_dn, b_dn):
    bsz, t, d = x.shape
    n_tok = bsz * t
    xf = x.reshape(n_tok, d)
    logits = (xf @ w_router).astype(F32) + b_router.astype(F32)
    top_v, top_i = lax.top_k(logits, TOP_K)
    gates = jax.nn.softmax(top_v, axis=-1)
    n_asg = n_tok * TOP_K
    e_flat = top_i.reshape(n_asg)
    tok_flat = jnp.arange(n_asg, dtype=jnp.int32) // TOP_K
    order = jnp.argsort(e_flat)
    e_sorted = e_flat[order]
    counts = jnp.bincount(e_flat, length=N_EXPERTS)
    padded = (counts + MOE_BLOCK - 1) // MOE_BLOCK * MOE_BLOCK
    pad_end = jnp.cumsum(padded)
    pad_start = pad_end - padded
    raw_start = jnp.cumsum(counts) - counts
    dest = pad_start[e_sorted] + jnp.arange(n_asg, dtype=jnp.int32) - raw_start[e_sorted]
    n_blocks = -(-n_asg // MOE_BLOCK) + N_EXPERTS
    n_slots = n_blocks * MOE_BLOCK
    slot_tok = jnp.zeros((n_slots,), jnp.int32).at[dest].set(tok_flat[order])
    slot_gate = jnp.zeros((n_slots,), F32).at[dest].set(gates.reshape(n_asg)[order])
    block_e = jnp.minimum(jnp.searchsorted(pad_end, jnp.arange(n_blocks) * MOE_BLOCK, side='right'), N_EXPERTS - 1)

    def expert_block(args):
        toks, e = args
        hgu = xf[toks] @ w_gu[e] + b_gu[e]
        gate, up = jnp.split(hgu, 2, axis=-1)
        gate = jnp.minimum(gate, SWIGLU_LIMIT)
        up = jnp.clip(up, -SWIGLU_LIMIT, SWIGLU_LIMIT)
        glu = gate * jax.nn.sigmoid(SWIGLU_ALPHA * gate)
        return ((up + 1.0) * glu) @ w_dn[e] + b_dn[e]

    out = lax.map(expert_block, (slot_tok.reshape(n_blocks, MOE_BLOCK), block_e))
    y = jnp.zeros((n_tok, d), F32).at[slot_tok].add(out.reshape(n_slots, d).astype(F32) * slot_gate[:, None])
    return y.reshape(bsz, t, d).astype(x.dtype)


def setup_inputs(seed: int = 0) -> dict:
    key = jax.random.key(seed)
    k = jax.random.split(key, 34)
    L = DEPTH

    def nrm(kk, shape, scale):
        return jax.random.normal(kk, shape, F32) * scale

    lam_u = jax.random.uniform(k[18], (L, 2, D_RNN), F32, 0.9, 0.999)
    lam_s = lam_u ** (1.0 / RG_C)
    return {
        'x_prompt': nrm(k[0], (BATCH, SEQ, D_MODEL), 1.0),
        'x_sample': nrm(k[1], (DEC_BATCH, DEC_SEQ, D_MODEL), 1.0),
        'cache_k': nrm(k[2], (DEC_BATCH, L, N_HEADS, PAST_LEN, HEAD_DIM), 1.0),
        'cache_v': nrm(k[3], (DEC_BATCH, L, N_HEADS, PAST_LEN, HEAD_DIM), 1.0),
        'state_h_fwd': nrm(k[4], (DEC_BATCH, L, D_RNN), 0.5),
        'state_h_bwd': nrm(k[5], (DEC_BATCH, L, D_RNN), 0.5),
        'c': nrm(k[6], (DEC_BATCH, D_MODEL), 1.0),
        'c_ctx': nrm(k[7], (D_MODEL,), 1.0),
        'w_ada': nrm(k[8], (L, D_MODEL, 6 * D_MODEL), 0.5 * D_MODEL ** -0.5),
        'b_ada': nrm(k[9], (L, 6 * D_MODEL), 0.02),
        'w_in': nrm(k[10], (L, D_MODEL, D_IN), D_MODEL ** -0.5),
        'b_in': nrm(k[11], (L, D_IN), 0.02),
        'conv_w': nrm(k[12], (L, CONV_W, D_RNN), CONV_W ** -0.5),
        'conv_b': nrm(k[13], (L, D_RNN), 0.02),
        'rg_wa': nrm(k[14], (L, 2, RNN_BLOCKS, RNN_BLOCK, RNN_BLOCK), RNN_BLOCK ** -0.5),
        'rg_ba': nrm(k[15], (L, 2, D_RNN), 0.02),
        'rg_wx': nrm(k[16], (L, 2, RNN_BLOCKS, RNN_BLOCK, RNN_BLOCK), RNN_BLOCK ** -0.5),
        'rg_bx': nrm(k[17], (L, 2, D_RNN), 0.02),
        'rg_lam': jnp.log(lam_s) - jnp.log1p(-lam_s),
        'rpb': nrm(k[19], (L, N_HEADS, 2 * WIN_ROWS - 1, 2 * WIN_COLS - 1), 0.1),
        'w_rnn_out': nrm(k[20], (L, D_RNN, D_MODEL), D_RNN ** -0.5),
        'w_na_out': nrm(k[21], (L, D_NA, D_MODEL), D_NA ** -0.5),
        'w_o': nrm(k[22], (L, D_MODEL, D_MODEL), BETA * D_MODEL ** -0.5),
        'b_o': nrm(k[23], (L, D_MODEL), 0.02),
        'ln1_g': 1.0 + nrm(k[24], (L, D_MODEL), 0.02),
        'ln1_b': nrm(k[25], (L, D_MODEL), 0.02),
        'ln2_g': 1.0 + nrm(k[26], (L, D_MODEL), 0.02),
        'ln2_b': nrm(k[27], (L, D_MODEL), 0.02),
        'w_router': nrm(k[28], (L, D_MODEL, N_EXPERTS), D_MODEL ** -0.5),
        'b_router': nrm(k[29], (L, N_EXPERTS), 0.01),
        'w_gu': nrm(k[30], (L, N_EXPERTS, D_MODEL, 2 * D_FF), D_MODEL ** -0.5),
        'b_gu': nrm(k[31], (L, N_EXPERTS, 2 * D_FF), 0.02),
        'w_dn': nrm(k[32], (L, N_EXPERTS, D_FF, D_MODEL), BETA * D_FF ** -0.5),
        'b_dn': nrm(k[33], (L, N_EXPERTS, D_MODEL), 0.02),
    }


def reference(x_prompt, x_sample, cache_k, cache_v, state_h_fwd, state_h_bwd, c, c_ctx,
              w_ada, b_ada, w_in, b_in, conv_w, conv_b, rg_wa, rg_ba, rg_wx, rg_bx, rg_lam, rpb,
              w_rnn_out, w_na_out, w_o, b_o, ln1_g, ln1_b, ln2_g, ln2_b,
              w_router, b_router, w_gu, b_gu, w_dn, b_dn):
    xp = x_prompt
    xs = x_sample
    ks, vs, hfs, hbs = [], [], [], []
    for l in range(DEPTH):
        lw = {
            'w_in': w_in[l], 'b_in': b_in[l], 'conv_w': conv_w[l], 'conv_b': conv_b[l],
            'rg_wa': rg_wa[l], 'rg_ba': rg_ba[l], 'rg_wx': rg_wx[l], 'rg_bx': rg_bx[l], 'rg_lam': rg_lam[l],
            'rpb': rpb[l], 'w_rnn_out': w_rnn_out[l], 'w_na_out': w_na_out[l], 'w_o': w_o[l], 'b_o': b_o[l],
        }
        moe_w = (w_router[l], b_router[l], w_gu[l], b_gu[l], w_dn[l], b_dn[l])

        sh1, sc1, g1, sh2, sc2, g2 = _adaln(c_ctx[None], w_ada[l], b_ada[l])
        u = xp * (1.0 + sc1) + sh1
        mix, k_c, v_c, hf_c, hb_c = _mixer_context(u, lw)
        xp = _layernorm(ALPHA * xp + g1 * mix, ln1_g[l], ln1_b[l])
        u = xp * (1.0 + sc2) + sh2
        xp = _layernorm(ALPHA * xp + g2 * _moe(u, *moe_w), ln2_g[l], ln2_b[l])
        ks.append(k_c)
        vs.append(v_c)
        hfs.append(hf_c)
        hbs.append(hb_c)

        sh1, sc1, g1, sh2, sc2, g2 = _adaln(c, w_ada[l], b_ada[l])
        u = xs * (1.0 + sc1) + sh1
        mix = _mixer_latent(u, cache_k[:, l], cache_v[:, l], state_h_fwd[:, l], state_h_bwd[:, l], lw)
        xs = _layernorm(ALPHA * xs + g1 * mix, ln1_g[l], ln1_b[l])
        u = xs * (1.0 + sc2) + sh2
        xs = _layernorm(ALPHA * xs + g2 * _moe(u, *moe_w), ln2_g[l], ln2_b[l])

    new_k = jnp.stack(ks, axis=1)
    new_v = jnp.stack(vs, axis=1)
    new_h_fwd = jnp.stack(hfs, axis=1)
    new_h_bwd = jnp.stack(hbs, axis=1)
    return (xp, xs, new_k, new_v, new_h_fwd, new_h_bwd)
```

```python
import functools

import numpy as np
import jax
import jax.numpy as jnp
from jax import lax
from jax.experimental import pallas as pl
from jax.experimental.pallas import tpu as pltpu

F32 = jnp.float32
BF16 = jnp.bfloat16

GRID_W = 64
WIN_ROWS = 8
WIN_COLS = 16
CONV_W = 4
RG_C = 8.0
TOP_K = 4
SWIGLU_LIMIT = 7.0
SWIGLU_ALPHA = 1.702
LN_EPS = 1e-5
NEG_INF = -1e30

VMEM_LIMIT_BYTES = 56 * 1024 * 1024
SUBLANES = 8
LANES = 128

NATTN_ROWS = 8
MOE_TM = 1024
MOE_TF = 256


def _cparams(sem):
    return pltpu.CompilerParams(dimension_semantics=sem, vmem_limit_bytes=VMEM_LIMIT_BYTES)


def _pick(n, pref):
    t = min(n, pref)
    while n % t:
        t //= 2
    return t


def _ada_kernel(c_ref, w_ref, b_ref, o_ref):
    c = c_ref[...]
    s = c * jax.nn.sigmoid(c)
    o_ref[...] = jnp.dot(s.astype(BF16), w_ref[...].astype(BF16), preferred_element_type=F32) + b_ref[...]


def _ada(cond, w_ada, b_ada):
    nb, d = cond.shape
    n = w_ada.shape[1]
    tn = _pick(n, 1024)
    return pl.pallas_call(
        _ada_kernel,
        out_shape=jax.ShapeDtypeStruct((nb, n), F32),
        grid=(n // tn,),
        in_specs=[
            pl.BlockSpec((nb, d), lambda j: (0, 0)),
            pl.BlockSpec((d, tn), lambda j: (0, j)),
            pl.BlockSpec((1, tn), lambda j: (0, j)),
        ],
        out_specs=pl.BlockSpec((nb, tn), lambda j: (0, j)),
        compiler_params=_cparams(("arbitrary",)),
        name="ada",
    )(cond, w_ada, b_ada.reshape(1, n))


def _inproj_kernel(x_ref, sh_ref, sc_ref, w_ref, b_ref, xr_ref, qkv_ref, g_ref, u_scr, *, n_xr, n_qkv):
    n = pl.program_id(1)

    @pl.when(n == 0)
    def _():
        u = x_ref[...] * (1.0 + sc_ref[0]) + sh_ref[0]
        u_scr[...] = u.astype(BF16)

    acc = jnp.dot(u_scr[...], w_ref[...], preferred_element_type=F32) + b_ref[...]

    @pl.when(n < n_xr)
    def _():
        xr_ref[...] = acc.astype(xr_ref.dtype)

    @pl.when((n >= n_xr) & (n < n_xr + n_qkv))
    def _():
        qkv_ref[...] = acc.astype(qkv_ref.dtype)

    @pl.when(n >= n_xr + n_qkv)
    def _():
        g_ref[...] = acc.astype(g_ref.dtype)


def _inproj(x, mod, w_in, b_in, d_rnn, d_na, tok_per_mod, qkv_dtype):
    n_tok, d = x.shape
    d_in = w_in.shape[1]
    tn = _pick(np.gcd(np.gcd(d_rnn, d_na), d), 1024)
    tm = _pick(tok_per_mod, 512)
    n_xr, n_qkv, n_g = d_rnn // tn, 3 * d_na // tn, 2 * d // tn
    assert (n_xr + n_qkv + n_g) * tn == d_in

    def mod_map(col):
        return lambda i, j: ((i * tm) // tok_per_mod, 0, col)

    return pl.pallas_call(
        functools.partial(_inproj_kernel, n_xr=n_xr, n_qkv=n_qkv),
        out_shape=(
            jax.ShapeDtypeStruct((n_tok, d_rnn), F32),
            jax.ShapeDtypeStruct((n_tok, 3 * d_na), qkv_dtype),
            jax.ShapeDtypeStruct((n_tok, 2 * d), BF16),
        ),
        grid=(n_tok // tm, d_in // tn),
        in_specs=[
            pl.BlockSpec((tm, d), lambda i, j: (i, 0)),
            pl.BlockSpec((1, 1, d), mod_map(0)),
            pl.BlockSpec((1, 1, d), mod_map(1)),
            pl.BlockSpec((d, tn), lambda i, j: (0, j)),
            pl.BlockSpec((1, tn), lambda i, j: (0, j)),
        ],
        out_specs=(
            pl.BlockSpec((tm, tn), lambda i, j: (i, jnp.minimum(j, n_xr - 1))),
            pl.BlockSpec((tm, tn), lambda i, j: (i, jnp.clip(j - n_xr, 0, n_qkv - 1))),
            pl.BlockSpec((tm, tn), lambda i, j: (i, jnp.clip(j - n_xr - n_qkv, 0, n_g - 1))),
        ),
        scratch_shapes=[pltpu.VMEM((tm, d), BF16)],
        compiler_params=_cparams(("arbitrary", "arbitrary")),
        name="inproj",
    )(x, mod, mod, w_in, b_in.reshape(1, d_in))


def _rglru_kernel(xf_ref, xfp_ref, xfn_ref, xb_ref, xbp_ref, xbn_ref,
                  cw_ref, cb_ref, w_ref, ba_ref, bx_ref, lam_ref, h0_ref,
                  yf_ref, yb_ref, hl_ref,
                  xext_scr, a_scr, b_scr, carry_scr, *, t_tile, n_t, seq_len, reset):
    j = pl.program_id(2)
    c = xf_ref.shape[-1]
    n_grp = t_tile // SUBLANES

    @pl.when(j == 0)
    def _():
        carry_scr[...] = h0_ref[0]

    row = lax.broadcasted_iota(jnp.int32, (t_tile, 1), 0)
    sub = row % SUBLANES

    def direction(d, x_ref, xp_ref, xn_ref, y_ref):
        jt = j if d == 0 else n_t - 1 - j
        xext_scr[0:SUBLANES, :] = jnp.where(jt > 0, xp_ref[0], 0.0)
        xext_scr[SUBLANES:SUBLANES + t_tile, :] = x_ref[0]
        xext_scr[SUBLANES + t_tile:2 * SUBLANES + t_tile, :] = jnp.where(jt < n_t - 1, xn_ref[0], 0.0)
        left = CONV_W // 2
        xc = cb_ref[...] + xext_scr[pl.ds(SUBLANES - left, t_tile), :] * cw_ref[0:1, :]
        for k in range(1, CONV_W):
            xc = xc + xext_scr[pl.ds(SUBLANES - left + k, t_tile), :] * cw_ref[k:k + 1, :]
        pre = jnp.dot(xc.astype(BF16), w_ref[d, 0], preferred_element_type=F32)
        r_gate = jax.nn.sigmoid(pre[:, :c] + ba_ref[d:d + 1, :])
        i_gate = jax.nn.sigmoid(pre[:, c:] + bx_ref[d:d + 1, :])
        lam = lam_ref[d:d + 1, :]
        softplus_neg_lam = jnp.maximum(-lam, 0.0) + jnp.log1p(jnp.exp(-jnp.abs(lam)))
        log_a = -RG_C * r_gate * softplus_neg_lam
        a = jnp.exp(log_a)
        mult = jnp.sqrt(1.0 - a * a)
        if reset:
            first = 0 if d == 0 else seq_len - 1
            mult = jnp.where(jt * t_tile + row == first, 1.0, mult)
        b = mult * i_gate * xc
        for s in (1, 2, 4):
            if d == 0:
                a_sh = pltpu.roll(a, s, axis=0)
                b_sh = pltpu.roll(b, s, axis=0)
                m = sub >= s
            else:
                a_sh = pltpu.roll(a, t_tile - s, axis=0)
                b_sh = pltpu.roll(b, t_tile - s, axis=0)
                m = sub < SUBLANES - s
            b = jnp.where(m, b + a * b_sh, b)
            a = jnp.where(m, a * a_sh, a)
        a_scr[...] = a
        b_scr[...] = b

        def body(g, h):
            gg = g if d == 0 else n_grp - 1 - g
            off = pl.multiple_of(gg * SUBLANES, SUBLANES)
            h_rows = b_scr[pl.ds(off, SUBLANES), :] + a_scr[pl.ds(off, SUBLANES), :] * h
            y_ref[0, pl.ds(off, SUBLANES), :] = h_rows.astype(y_ref.dtype)
            return h_rows[SUBLANES - 1:SUBLANES, :] if d == 0 else h_rows[0:1, :]

        h = lax.fori_loop(0, n_grp, body, carry_scr[d:d + 1, :], unroll=4)
        carry_scr[d:d + 1, :] = h
        hl_ref[0, d:d + 1, :] = h

    direction(0, xf_ref, xfp_ref, xfn_ref, yf_ref)
    direction(1, xb_ref, xbp_ref, xbn_ref, yb_ref)


def _rglru(xr, conv_w, conv_b, w_gates, rg_ba, rg_bx, rg_lam, h0, reset):
    bsz, t, d_rnn = xr.shape
    n_blk, c = w_gates.shape[1], w_gates.shape[2]
    t_tile = _pick(t, 1024)
    n_t = t // t_tile
    tb = t_tile // SUBLANES
    last_blk = t // SUBLANES - 1

    def main(rev):
        return lambda b, n, j: (b, (n_t - 1 - j) if rev else j, n)

    def prev(rev):
        return lambda b, n, j: (b, jnp.maximum(((n_t - 1 - j) if rev else j) * tb - 1, 0), n)

    def nxt(rev):
        return lambda b, n, j: (b, jnp.minimum((((n_t - 1 - j) if rev else j) + 1) * tb, last_blk), n)

    vec = lambda rows: pl.BlockSpec((rows, c), lambda b, n, j: (0, n))
    return pl.pallas_call(
        functools.partial(_rglru_kernel, t_tile=t_tile, n_t=n_t, seq_len=t, reset=reset),
        out_shape=(
            jax.ShapeDtypeStruct((bsz, t, d_rnn), BF16),
            jax.ShapeDtypeStruct((bsz, t, d_rnn), BF16),
            jax.ShapeDtypeStruct((bsz, 2, d_rnn), F32),
        ),
        grid=(bsz, n_blk, n_t),
        in_specs=[
            pl.BlockSpec((1, t_tile, c), main(False)),
            pl.BlockSpec((1, SUBLANES, c), prev(False)),
            pl.BlockSpec((1, SUBLANES, c), nxt(False)),
            pl.BlockSpec((1, t_tile, c), main(True)),
            pl.BlockSpec((1, SUBLANES, c), prev(True)),
            pl.BlockSpec((1, SUBLANES, c), nxt(True)),
            vec(CONV_W),
            vec(1),
            pl.BlockSpec((2, 1, c, 2 * c), lambda b, n, j: (0, n, 0, 0)),
            vec(2), vec(2), vec(2),
            pl.BlockSpec((1, 2, c), lambda b, n, j: (b, 0, n)),
        ],
        out_specs=(
            pl.BlockSpec((1, t_tile, c), main(False)),
            pl.BlockSpec((1, t_tile, c), main(True)),
            pl.BlockSpec((1, 2, c), lambda b, n, j: (b, 0, n)),
        ),
        scratch_shapes=[
            pltpu.VMEM((t_tile + 2 * SUBLANES, c), F32),
            pltpu.VMEM((t_tile, c), F32),
            pltpu.VMEM((t_tile, c), F32),
            pltpu.VMEM((2, c), F32),
        ],
        compiler_params=_cparams(("arbitrary", "arbitrary", "arbitrary")),
        name="rglru",
    )(xr, xr, xr, xr, xr, xr, conv_w, conv_b.reshape(1, d_rnn), w_gates, rg_ba, rg_bx, rg_lam, h0)


def _qk(q, k):
    return lax.dot_general(q, k, (((1,), (1,)), ((), ())), preferred_element_type=F32)


def _cattn_kernel(q_ref, k_ref, v_ref, o_ref, *, n_heads, hd, scale):
    for h in range(n_heads):
        sl = slice(h * hd, (h + 1) * hd)
        q = q_ref[0, :, sl].astype(BF16)
        k = k_ref[0, :, sl].astype(BF16)
        v = v_ref[0, :, sl].astype(BF16)
        s = _qk(q, k) * scale
        p = jnp.exp(s - jnp.max(s, axis=-1, keepdims=True))
        l = jnp.sum(p, axis=-1, keepdims=True)
        o = jnp.dot(p.astype(BF16), v, preferred_element_type=F32) / l
        o_ref[0, :, sl] = o.astype(o_ref.dtype)


def _cattn(qkv, n_heads, hd):
    bsz, s, _ = qkv.shape
    d_na = n_heads * hd
    spec = lambda col: pl.BlockSpec((1, s, d_na), lambda b: (b, 0, col))
    return pl.pallas_call(
        functools.partial(_cattn_kernel, n_heads=n_heads, hd=hd, scale=hd ** -0.5),
        out_shape=jax.ShapeDtypeStruct((bsz, s, d_na), BF16),
        grid=(bsz,),
        in_specs=[spec(0), spec(1), spec(2)],
        out_specs=pl.BlockSpec((1, s, d_na), lambda b: (b, 0, 0)),
        compiler_params=_cparams(("arbitrary",)),
        name="cattn",
    )(qkv, qkv, qkv)


def _nattn_window_start(rb, rows, r_blk, w_blk):
    return jnp.clip(rb * r_blk - WIN_ROWS // 2, 0, rows - w_blk)


def _nattn_kernel(q_ref, k_ref, v_ref, ck_ref, cv_ref, bias_ref, o_ref, *, hd, scale, rows, r_blk, w_blk):
    rb = pl.program_id(1)
    w0 = _nattn_window_start(rb, rows, r_blk, w_blk)
    start = pl.multiple_of(w0 * GRID_W, GRID_W)
    k_win = k_ref[0, pl.ds(start, w_blk * GRID_W), :]
    v_win = v_ref[0, pl.ds(start, w_blk * GRID_W), :]
    q = q_ref[0]
    outs = []
    for h in range(q.shape[-1] // hd):
        sl = slice(h * hd, (h + 1) * hd)
        qh = q[:, sl]
        s_lat = _qk(qh, k_win[:, sl]) * scale + bias_ref[h, 0]
        s_ctx = _qk(qh, ck_ref[0, h].astype(BF16)) * scale
        m = jnp.maximum(jnp.max(s_lat, axis=-1, keepdims=True), jnp.max(s_ctx, axis=-1, keepdims=True))
        p_lat = jnp.exp(s_lat - m)
        p_ctx = jnp.exp(s_ctx - m)
        l = jnp.sum(p_lat, axis=-1, keepdims=True) + jnp.sum(p_ctx, axis=-1, keepdims=True)
        o = (jnp.dot(p_lat.astype(BF16), v_win[:, sl], preferred_element_type=F32)
             + jnp.dot(p_ctx.astype(BF16), cv_ref[0, h].astype(BF16), preferred_element_type=F32))
        outs.append(o / l)
    o_ref[0] = jnp.concatenate(outs, axis=-1).astype(o_ref.dtype)


def _nattn_bias(rpb, rows, r_blk, w_blk):
    n_rb = rows // r_blk
    wr = min(WIN_ROWS, rows)
    pats = []
    for rb in (0, min(1, n_rb - 1), n_rb - 1):
        r0 = rb * r_blk
        w0 = int(np.clip(r0 - WIN_ROWS // 2, 0, rows - w_blk))
        qi = np.arange(r_blk * GRID_W)
        ki = np.arange(w_blk * GRID_W)
        qrow, qcol = r0 + qi // GRID_W, qi % GRID_W
        krow, kcol = w0 + ki // GRID_W, ki % GRID_W
        rstart = np.clip(qrow - wr // 2, 0, rows - wr)
        cstart = np.clip(qcol - WIN_COLS // 2, 0, GRID_W - WIN_COLS)
        ok = ((krow[None, :] >= rstart[:, None]) & (krow[None, :] < rstart[:, None] + wr)
              & (kcol[None, :] >= cstart[:, None]) & (kcol[None, :] < cstart[:, None] + WIN_COLS))
        dr = np.clip(krow[None, :] - qrow[:, None] + WIN_ROWS - 1, 0, 2 * WIN_ROWS - 2)
        dc = np.clip(kcol[None, :] - qcol[:, None] + WIN_COLS - 1, 0, 2 * WIN_COLS - 2)
        pats.append(jnp.where(ok[None], rpb[:, dr, dc].astype(F32), NEG_INF))
    return jnp.stack(pats, axis=1)


def _nattn(qkv, ck, cv, rpb, n_heads, hd):
    bsz, t, _ = qkv.shape
    d_na = n_heads * hd
    rows = t // GRID_W
    r_blk = min(NATTN_ROWS, rows)
    w_blk = min(r_blk + WIN_ROWS - 1, rows)
    n_rb = rows // r_blk
    for rb in range(1, n_rb - 1):
        assert 0 <= rb * r_blk - WIN_ROWS // 2 <= rows - w_blk
    hp = LANES // hd
    n_pair = d_na // LANES
    past = ck.shape[2]
    bias = _nattn_bias(rpb, rows, r_blk, w_blk)
    tq, tk = r_blk * GRID_W, w_blk * GRID_W

    def pat(rb):
        return jnp.where(rb == 0, 0, jnp.where(rb == n_rb - 1, 2, 1))

    return pl.pallas_call(
        functools.partial(_nattn_kernel, hd=hd, scale=hd ** -0.5, rows=rows, r_blk=r_blk, w_blk=w_blk),
        out_shape=jax.ShapeDtypeStruct((bsz, t, d_na), BF16),
        grid=(n_pair, n_rb, bsz),
        in_specs=[
            pl.BlockSpec((1, tq, LANES), lambda p, rb, b: (b, rb, p)),
            pl.BlockSpec((1, t, LANES), lambda p, rb, b: (b, 0, n_pair + p)),
            pl.BlockSpec((1, t, LANES), lambda p, rb, b: (b, 0, 2 * n_pair + p)),
            pl.BlockSpec((1, hp, past, hd), lambda p, rb, b: (b, p, 0, 0)),
            pl.BlockSpec((1, hp, past, hd), lambda p, rb, b: (b, p, 0, 0)),
            pl.BlockSpec((hp, 1, tq, tk), lambda p, rb, b: (p, pat(rb), 0, 0)),
        ],
        out_specs=pl.BlockSpec((1, tq, LANES), lambda p, rb, b: (b, rb, p)),
        compiler_params=_cparams(("arbitrary", "arbitrary", "arbitrary")),
        name="nattn",
    )(qkv, qkv, qkv, ck, cv, bias)


def _merge_kernel(yf_ref, yb_ref, ya_ref, ga_ref, gb_ref, wr_ref, wa_ref, o_ref):
    y_rnn = (yf_ref[...].astype(F32) + yb_ref[...].astype(F32)).astype(BF16)
    t_rnn = jnp.dot(y_rnn, wr_ref[...], preferred_element_type=F32)
    t_na = jnp.dot(ya_ref[...], wa_ref[...], preferred_element_type=F32)
    merged = (jax.nn.sigmoid(ga_ref[...].astype(F32)) * t_rnn
              + jax.nn.sigmoid(gb_ref[...].astype(F32)) * t_na)
    o_ref[...] = merged.astype(o_ref.dtype)


def _merge(yf, yb, ya, g, w_rnn_out, w_na_out):
    n_tok, d_rnn = yf.shape
    d_na = ya.shape[1]
    d = w_rnn_out.shape[1]
    tm = _pick(n_tok, 512)
    tn = _pick(d, 1024)
    n_n = d // tn
    return pl.pallas_call(
        _merge_kernel,
        out_shape=jax.ShapeDtypeStruct((n_tok, d), BF16),
        grid=(n_tok // tm, n_n),
        in_specs=[
            pl.BlockSpec((tm, d_rnn), lambda i, j: (i, 0)),
            pl.BlockSpec((tm, d_rnn), lambda i, j: (i, 0)),
            pl.BlockSpec((tm, d_na), lambda i, j: (i, 0)),
            pl.BlockSpec((tm, tn), lambda i, j: (i, j)),
            pl.BlockSpec((tm, tn), lambda i, j: (i, n_n + j)),
            pl.BlockSpec((d_rnn, tn), lambda i, j: (0, j)),
            pl.BlockSpec((d_na, tn), lambda i, j: (0, j)),
        ],
        out_specs=pl.BlockSpec((tm, tn), lambda i, j: (i, j)),
        compiler_params=_cparams(("arbitrary", "arbitrary")),
        name="merge",
    )(yf, yb, ya, g, g, w_rnn_out, w_na_out)


def _layernorm(z, g, b):
    mu = jnp.mean(z, axis=-1, keepdims=True)
    zc = z - mu
    var = jnp.mean(zc * zc, axis=-1, keepdims=True)
    return zc * lax.rsqrt(var + LN_EPS) * g + b


def _post1_kernel(m_ref, x_ref, g1_ref, sh2_ref, sc2_ref, wo_ref, bo_ref, lng_ref, lnb_ref, wrt_ref, brt_ref,
                  x1_ref, u2_ref, ti_ref, tg_ref, *, alpha, n_experts):
    mix = jnp.dot(m_ref[...], wo_ref[...], preferred_element_type=F32) + bo_ref[...]
    x1 = _layernorm(alpha * x_ref[...] + g1_ref[0] * mix, lng_ref[...], lnb_ref[...])
    x1_ref[...] = x1
    u2 = x1 * (1.0 + sc2_ref[0]) + sh2_ref[0]
    u2_ref[...] = u2
    hi = u2.astype(BF16)
    lo = (u2 - hi.astype(F32)).astype(BF16)
    ph = jnp.dot(hi, wrt_ref[...], preferred_element_type=F32)
    plo = jnp.dot(lo, wrt_ref[...], preferred_element_type=F32)
    logits = (ph[:, :LANES] + ph[:, LANES:]) + (plo[:, :LANES] + plo[:, LANES:]) + brt_ref[...]
    lane = lax.broadcasted_iota(jnp.int32, logits.shape, 1)
    lane_f = lane.astype(F32)
    logits = jnp.where(lane < n_experts, logits, NEG_INF)
    top_i = jnp.zeros(logits.shape, jnp.int32)
    top_e = jnp.zeros(logits.shape, F32)
    v0 = None
    denom = None
    for k in range(TOP_K):
        v = jnp.max(logits, axis=-1, keepdims=True)
        idx = jnp.min(jnp.where(logits == v, lane_f, float(LANES)), axis=-1, keepdims=True).astype(jnp.int32)
        if k == 0:
            v0 = v
        e = jnp.exp(v - v0)
        denom = e if k == 0 else denom + e
        top_i = jnp.where(lane == k, idx, top_i)
        top_e = jnp.where(lane == k, e, top_e)
        logits = jnp.where(lane == idx, NEG_INF, logits)
    ti_ref[...] = top_i
    tg_ref[...] = top_e / denom


def _post1(merged, x, mod, w_o, b_o, ln_g, ln_b, w_rt, b_rt, tok_per_mod, alpha, n_experts):
    n_tok, d = x.shape
    tm = _pick(tok_per_mod, 256)

    def mod_map(col):
        return lambda i: ((i * tm) // tok_per_mod, 0, col)

    row = lambda a: pl.BlockSpec((1, a.shape[-1]), lambda i: (0, 0))
    tile = pl.BlockSpec((tm, d), lambda i: (i, 0))
    small = pl.BlockSpec((tm, LANES), lambda i: (i, 0))
    return pl.pallas_call(
        functools.partial(_post1_kernel, alpha=alpha, n_experts=n_experts),
        out_shape=(
            jax.ShapeDtypeStruct((n_tok, d), F32),
            jax.ShapeDtypeStruct((n_tok, d), F32),
            jax.ShapeDtypeStruct((n_tok, LANES), jnp.int32),
            jax.ShapeDtypeStruct((n_tok, LANES), F32),
        ),
        grid=(n_tok // tm,),
        in_specs=[
            tile, tile,
            pl.BlockSpec((1, 1, d), mod_map(2)),
            pl.BlockSpec((1, 1, d), mod_map(3)),
            pl.BlockSpec((1, 1, d), mod_map(4)),
            pl.BlockSpec((d, d), lambda i: (0, 0)),
            row(b_o), row(ln_g), row(ln_b),
            pl.BlockSpec((d, 2 * LANES), lambda i: (0, 0)),
            row(b_rt),
        ],
        out_specs=(tile, tile, small, small),
        compiler_params=_cparams(("arbitrary",)),
        name="post1",
    )(merged, x, mod, mod, mod, w_o, b_o, ln_g, ln_b, w_rt, b_rt)


def _moe_kernel(be_ref, bv_ref, tok_ref, dst_ref, u_hbm, wg_ref, wu_ref, wd_ref, bg_ref, bu_ref, bd_ref,
                y_hbm, x_scr, xb_scr, acc_scr, gsem, ssem, *, tm, n_f):
    i = pl.program_id(0)
    f = pl.program_id(1)
    valid = bv_ref[i] > 0

    def gather_copy(r):
        return pltpu.make_async_copy(u_hbm.at[pl.ds(tok_ref[0, 0, r], 1), :], x_scr.at[pl.ds(r, 1), :], gsem)

    def scatter_copy(r):
        return pltpu.make_async_copy(acc_scr.at[pl.ds(r, 1), :], y_hbm.at[pl.ds(dst_ref[0, 0, r], 1), :], ssem)

    @pl.when(valid & (f == 0))
    def _():
        def start(r, _):
            gather_copy(r).start()
            return 0
        lax.fori_loop(0, tm, start, 0)

        def wait(r, _):
            gather_copy(r).wait()
            return 0
        lax.fori_loop(0, tm, wait, 0)
        xb_scr[...] = x_scr[...].astype(BF16)

    @pl.when(valid)
    def _():
        xb = xb_scr[...]
        gate = jnp.dot(xb, wg_ref[0].astype(BF16), preferred_element_type=F32) + bg_ref[0]
        up = jnp.dot(xb, wu_ref[0].astype(BF16), preferred_element_type=F32) + bu_ref[0]
        gate = jnp.minimum(gate, SWIGLU_LIMIT)
        up = jnp.clip(up, -SWIGLU_LIMIT, SWIGLU_LIMIT)
        glu = gate * jax.nn.sigmoid(SWIGLU_ALPHA * gate)
        act = ((up + 1.0) * glu).astype(BF16)
        part = jnp.dot(act, wd_ref[0].astype(BF16), preferred_element_type=F32)

        @pl.when(f == 0)
        def _():
            acc_scr[...] = part + bd_ref[0]

        @pl.when(f > 0)
        def _():
            acc_scr[...] += part

    @pl.when(valid & (f == n_f - 1))
    def _():
        def start(r, _):
            scatter_copy(r).start()
            return 0
        lax.fori_loop(0, tm, start, 0)

        def wait(r, _):
            scatter_copy(r).wait()
            return 0
        lax.fori_loop(0, tm, wait, 0)


def _moe(u, top_i, w_gu, b_gu, w_dn, b_dn):
    n_tok, d = u.shape
    n_exp, _, two_ff = w_gu.shape
    d_ff = two_ff // 2
    n_asg = n_tok * TOP_K
    tm = _pick(n_asg, MOE_TM)
    tf = _pick(d_ff, MOE_TF)
    n_f = d_ff // tf
    n_blocks = -(-n_asg // tm) + n_exp
    n_slots = n_blocks * tm

    e_flat = top_i.reshape(n_asg)
    order = jnp.argsort(e_flat).astype(jnp.int32)
    e_sorted = e_flat[order]
    counts = jnp.bincount(e_flat, length=n_exp).astype(jnp.int32)
    padded = (counts + tm - 1) // tm * tm
    pad_end = jnp.cumsum(padded)
    pad_start = pad_end - padded
    raw_start = jnp.cumsum(counts) - counts
    dest = pad_start[e_sorted] + jnp.arange(n_asg, dtype=jnp.int32) - raw_start[e_sorted]
    slot = jnp.arange(n_slots, dtype=jnp.int32)
    slot_tok = jnp.zeros((n_slots,), jnp.int32).at[dest].set(order // TOP_K)
    slot_dst = (n_asg + slot % tm).at[dest].set(order)
    blk_start = jnp.arange(n_blocks, dtype=jnp.int32) * tm
    blk_e = jnp.minimum(jnp.searchsorted(pad_end, blk_start, side='right'), n_exp - 1).astype(jnp.int32)
    blk_valid = (blk_start < pad_end[-1]).astype(jnp.int32)

    def fe(f, bv, i):
        return jnp.where(bv[i] > 0, f, n_f - 1)

    idx_spec = pl.BlockSpec((1, 1, tm), lambda i, f, be, bv: (i, 0, 0), memory_space=pltpu.SMEM)
    grid_spec = pltpu.PrefetchScalarGridSpec(
        num_scalar_prefetch=2,
        grid=(n_blocks, n_f),
        in_specs=[
            idx_spec, idx_spec,
            pl.BlockSpec(memory_space=pl.ANY),
            pl.BlockSpec((1, d, tf), lambda i, f, be, bv: (be[i], 0, fe(f, bv, i))),
            pl.BlockSpec((1, d, tf), lambda i, f, be, bv: (be[i], 0, n_f + fe(f, bv, i))),
            pl.BlockSpec((1, tf, d), lambda i, f, be, bv: (be[i], fe(f, bv, i), 0)),
            pl.BlockSpec((1, 1, tf), lambda i, f, be, bv: (be[i], 0, fe(f, bv, i))),
            pl.BlockSpec((1, 1, tf), lambda i, f, be, bv: (be[i], 0, n_f + fe(f, bv, i))),
            pl.BlockSpec((1, 1, d), lambda i, f, be, bv: (be[i], 0, 0)),
        ],
        out_specs=pl.BlockSpec(memory_space=pl.ANY),
        scratch_shapes=[
            pltpu.VMEM((tm, d), F32),
            pltpu.VMEM((tm, d), BF16),
            pltpu.VMEM((tm, d), F32),
            pltpu.SemaphoreType.DMA,
            pltpu.SemaphoreType.DMA,
        ],
    )
    return pl.pallas_call(
        functools.partial(_moe_kernel, tm=tm, n_f=n_f),
        out_shape=jax.ShapeDtypeStruct((n_asg + tm, d), F32),
        grid_spec=grid_spec,
        compiler_params=_cparams(("arbitrary", "arbitrary")),
        name="moe",
    )(blk_e, blk_valid, slot_tok.reshape(n_blocks, 1, tm), slot_dst.reshape(n_blocks, 1, tm), u,
      w_gu, w_gu, w_dn, b_gu.reshape(n_exp, 1, two_ff), b_gu.reshape(n_exp, 1, two_ff), b_dn.reshape(n_exp, 1, d))


def _post2_kernel(yk_ref, tg_ref, x_ref, g2_ref, lng_ref, lnb_ref, o_ref, *, alpha, d):
    tg = tg_ref[...]
    y = tg[:, 0:1] * yk_ref[:, 0:d]
    for k in range(1, TOP_K):
        y = y + tg[:, k:k + 1] * yk_ref[:, k * d:(k + 1) * d]
    o_ref[...] = _layernorm(alpha * x_ref[...] + g2_ref[0] * y, lng_ref[...], lnb_ref[...])


def _post2(yk, top_g, x1, mod, ln_g, ln_b, tok_per_mod, alpha, tok_offset):
    n_tok, d = x1.shape
    tm = _pick(tok_per_mod, 256)
    off = tok_offset // tm
    assert off * tm == tok_offset
    row = lambda a: pl.BlockSpec((1, a.shape[-1]), lambda i: (0, 0))
    return pl.pallas_call(
        functools.partial(_post2_kernel, alpha=alpha, d=d),
        out_shape=jax.ShapeDtypeStruct((n_tok, d), F32),
        grid=(n_tok // tm,),
        in_specs=[
            pl.BlockSpec((tm, TOP_K * d), lambda i: (off + i, 0)),
            pl.BlockSpec((tm, LANES), lambda i: (i, 0)),
            pl.BlockSpec((tm, d), lambda i: (i, 0)),
            pl.BlockSpec((1, 1, d), lambda i: ((i * tm) // tok_per_mod, 0, 5)),
            row(ln_g), row(ln_b),
        ],
        out_specs=pl.BlockSpec((tm, d), lambda i: (i, 0)),
        compiler_params=_cparams(("arbitrary",)),
        name="post2",
    )(yk, top_g, x1, mod, ln_g, ln_b)


def _router_weights(w_router, b_router):
    d, n_exp = w_router.shape
    hi = w_router.astype(BF16)
    lo = (w_router - hi.astype(F32)).astype(BF16)
    pad = lambda a: jnp.pad(a, ((0, 0), (0, LANES - n_exp)))
    return jnp.concatenate([pad(hi), pad(lo)], axis=1), jnp.pad(b_router, (0, LANES - n_exp)).reshape(1, LANES)


def kernel(x_prompt, x_sample, cache_k, cache_v, state_h_fwd, state_h_bwd, c, c_ctx, w_ada, b_ada, w_in, b_in,
           conv_w, conv_b, rg_wa, rg_ba, rg_wx, rg_bx, rg_lam, rpb, w_rnn_out, w_na_out, w_o, b_o,
           ln1_g, ln1_b, ln2_g, ln2_b, w_router, b_router, w_gu, b_gu, w_dn, b_dn):
    depth, d, _ = w_in.shape
    bsz, seq, _ = x_prompt.shape
    dbsz, dseq, _ = x_sample.shape
    n_heads, hd = cache_k.shape[2], cache_k.shape[4]
    d_na = n_heads * hd
    d_rnn = conv_w.shape[2]
    n_exp = w_router.shape[2]
    alpha = (2 * depth) ** 0.25
    n_p, n_s = bsz * seq, dbsz * dseq

    xp = x_prompt.reshape(n_p, d)
    xs = x_sample.reshape(n_s, d)
    n_cond = -(-(1 + dbsz) // SUBLANES) * SUBLANES
    cond = jnp.concatenate([c_ctx[None], c, jnp.zeros((n_cond - 1 - dbsz, d), F32)], axis=0)
    ks, vs, hfs, hbs = [], [], [], []
    for l in range(depth):
        mod = _ada(cond, w_ada[l], b_ada[l]).reshape(n_cond, 1, 6 * d)
        mod_p, mod_s = mod[0:1], mod[1:1 + dbsz]
        w_in_l = w_in[l].astype(BF16)
        w_gates = jnp.concatenate([rg_wa[l], rg_wx[l]], axis=-1).astype(BF16)
        w_rnn_l, w_na_l, w_o_l = w_rnn_out[l].astype(BF16), w_na_out[l].astype(BF16), w_o[l].astype(BF16)
        w_rt, b_rt = _router_weights(w_router[l], b_router[l])
        row = lambda a: a[l].reshape(1, -1)

        def mixer(x, mod_g, n_b, t, h0, tok_per_mod, is_ctx):
            xr, qkv, g = _inproj(x, mod_g, w_in_l, b_in[l], d_rnn, d_na, tok_per_mod, F32 if is_ctx else BF16)
            yf, yb, h_last = _rglru(xr.reshape(n_b, t, d_rnn), conv_w[l], conv_b[l], w_gates,
                                    rg_ba[l], rg_bx[l], rg_lam[l], h0, is_ctx)
            qkv3 = qkv.reshape(n_b, t, 3 * d_na)
            if is_ctx:
                ya = _cattn(qkv3, n_heads, hd)
            else:
                ya = _nattn(qkv3, cache_k[:, l], cache_v[:, l], rpb[l], n_heads, hd)
            merged = _merge(yf.reshape(n_b * t, d_rnn), yb.reshape(n_b * t, d_rnn), ya.reshape(n_b * t, d_na),
                            g, w_rnn_l, w_na_l)
            x1, u2, ti, tg = _post1(merged, x, mod_g, w_o_l, row(b_o), row(ln1_g), row(ln1_b), w_rt, b_rt,
                                    tok_per_mod, alpha, n_exp)
            return x1, u2, ti, tg, qkv3, h_last

        x1p, u2p, tip, tgp, qkv_p, hl_p = mixer(xp, mod_p, bsz, seq, jnp.zeros((bsz, 2, d_rnn), F32), n_p, True)
        h0_s = jnp.stack([state_h_fwd[:, l], state_h_bwd[:, l]], axis=1)
        x1s, u2s, tis, tgs, _, _ = mixer(xs, mod_s, dbsz, dseq, h0_s, dseq, False)

        u2 = jnp.concatenate([u2p, u2s], axis=0)
        top_i = jnp.concatenate([tip[:, :TOP_K], tis[:, :TOP_K]], axis=0)
        yk = _moe(u2, top_i, w_gu[l], b_gu[l], w_dn[l], b_dn[l])
        yk = yk.reshape(-1, TOP_K * d)
        xp = _post2(yk, tgp, x1p, mod_p, row(ln2_g), row(ln2_b), n_p, alpha, 0)
        xs = _post2(yk, tgs, x1s, mod_s, row(ln2_g), row(ln2_b), dseq, alpha, n_p)

        heads = lambda a: a.reshape(bsz, seq, n_heads, hd).transpose(0, 2, 1, 3)
        ks.append(heads(qkv_p[:, :, d_na:2 * d_na]))
        vs.append(heads(qkv_p[:, :, 2 * d_na:]))
        hfs.append(hl_p[:, 0])
        hbs.append(hl_p[:, 1])

    return (xp.reshape(bsz, seq, d), xs.reshape(dbsz, dseq, d),
            jnp.stack(ks, axis=1), jnp.stack(vs, axis=1), jnp.stack(hfs, axis=1), jnp.stack(hbs, axis=1))
```

```python
import functools

import numpy as np
import jax
import jax.numpy as jnp
from jax import lax
from jax.experimental import pallas as pl
from jax.experimental.pallas import tpu as pltpu

F32 = jnp.float32
BF16 = jnp.bfloat16

GRID_W = 64
WIN_ROWS = 8
WIN_COLS = 16
CONV_W = 4
RG_C = 8.0
TOP_K = 4
SWIGLU_LIMIT = 7.0
SWIGLU_ALPHA = 1.702
LN_EPS = 1e-5
NEG_INF = -1e30

VMEM_LIMIT_BYTES = 56 * 1024 * 1024
SUBLANES = 8
LANES = 128

NATTN_ROWS = 8
MOE_TT = 256
MOE_TM = 1024
MOE_TF = 256


def _cparams(sem):
    return pltpu.CompilerParams(dimension_semantics=sem, vmem_limit_bytes=VMEM_LIMIT_BYTES)


def _pick(n, pref):
    t = min(n, pref)
    while n % t:
        t //= 2
    return t


def _ada_kernel(c_ref, w_ref, b_ref, o_ref):
    c = c_ref[...]
    s = c * jax.nn.sigmoid(c)
    o_ref[...] = jnp.dot(s.astype(BF16), w_ref[...].astype(BF16), preferred_element_type=F32) + b_ref[...]


def _ada(cond, w_ada, b_ada):
    nb, d = cond.shape
    n = w_ada.shape[1]
    tn = _pick(n, 1024)
    return pl.pallas_call(
        _ada_kernel,
        out_shape=jax.ShapeDtypeStruct((nb, n), F32),
        grid=(n // tn,),
        in_specs=[
            pl.BlockSpec((nb, d), lambda j: (0, 0)),
            pl.BlockSpec((d, tn), lambda j: (0, j)),
            pl.BlockSpec((1, tn), lambda j: (0, j)),
        ],
        out_specs=pl.BlockSpec((nb, tn), lambda j: (0, j)),
        compiler_params=_cparams(("arbitrary",)),
        name="ada",
    )(cond, w_ada, b_ada.reshape(1, n))


def _inproj_kernel(x_ref, sh_ref, sc_ref, w_ref, b_ref, xr_ref, qkv_ref, g_ref, u_scr, *, n_xr, n_qkv):
    n = pl.program_id(1)

    @pl.when(n == 0)
    def _():
        u = x_ref[...] * (1.0 + sc_ref[0]) + sh_ref[0]
        u_scr[...] = u.astype(BF16)

    acc = jnp.dot(u_scr[...], w_ref[...], preferred_element_type=F32) + b_ref[...]

    @pl.when(n < n_xr)
    def _():
        xr_ref[...] = acc.astype(xr_ref.dtype)

    @pl.when((n >= n_xr) & (n < n_xr + n_qkv))
    def _():
        qkv_ref[...] = acc.astype(qkv_ref.dtype)

    @pl.when(n >= n_xr + n_qkv)
    def _():
        g_ref[...] = acc.astype(g_ref.dtype)


def _inproj(x, mod, w_in, b_in, d_rnn, d_na, tok_per_mod, qkv_dtype):
    n_tok, d = x.shape
    d_in = w_in.shape[1]
    tn = _pick(np.gcd(np.gcd(d_rnn, d_na), d), 1024)
    tm = _pick(tok_per_mod, 512)
    n_xr, n_qkv, n_g = d_rnn // tn, 3 * d_na // tn, 2 * d // tn
    assert (n_xr + n_qkv + n_g) * tn == d_in

    def mod_map(col):
        return lambda i, j: ((i * tm) // tok_per_mod, 0, col)

    return pl.pallas_call(
        functools.partial(_inproj_kernel, n_xr=n_xr, n_qkv=n_qkv),
        out_shape=(
            jax.ShapeDtypeStruct((n_tok, d_rnn), F32),
            jax.ShapeDtypeStruct((n_tok, 3 * d_na), qkv_dtype),
            jax.ShapeDtypeStruct((n_tok, 2 * d), BF16),
        ),
        grid=(n_tok // tm, d_in // tn),
        in_specs=[
            pl.BlockSpec((tm, d), lambda i, j: (i, 0)),
            pl.BlockSpec((1, 1, d), mod_map(0)),
            pl.BlockSpec((1, 1, d), mod_map(1)),
            pl.BlockSpec((d, tn), lambda i, j: (0, j)),
            pl.BlockSpec((1, tn), lambda i, j: (0, j)),
        ],
        out_specs=(
            pl.BlockSpec((tm, tn), lambda i, j: (i, jnp.minimum(j, n_xr - 1))),
            pl.BlockSpec((tm, tn), lambda i, j: (i, jnp.clip(j - n_xr, 0, n_qkv - 1))),
            pl.BlockSpec((tm, tn), lambda i, j: (i, jnp.clip(j - n_xr - n_qkv, 0, n_g - 1))),
        ),
        scratch_shapes=[pltpu.VMEM((tm, d), BF16)],
        compiler_params=_cparams(("arbitrary", "arbitrary")),
        name="inproj",
    )(x, mod, mod, w_in, b_in.reshape(1, d_in))


def _rglru_kernel(xf_ref, xfp_ref, xfn_ref, xb_ref, xbp_ref, xbn_ref,
                  cw_ref, cb_ref, w_ref, ba_ref, bx_ref, lam_ref, h0_ref,
                  yf_ref, yb_ref, hl_ref,
                  xext_scr, a_scr, b_scr, carry_scr, *, t_tile, n_t, seq_len, reset):
    j = pl.program_id(2)
    c = xf_ref.shape[-1]
    n_grp = t_tile // SUBLANES

    @pl.when(j == 0)
    def _():
        carry_scr[...] = h0_ref[0]

    row = lax.broadcasted_iota(jnp.int32, (t_tile, 1), 0)
    sub = row % SUBLANES

    def direction(d, x_ref, xp_ref, xn_ref, y_ref):
        jt = j if d == 0 else n_t - 1 - j
        xext_scr[0:SUBLANES, :] = jnp.where(jt > 0, xp_ref[0], 0.0)
        xext_scr[SUBLANES:SUBLANES + t_tile, :] = x_ref[0]
        xext_scr[SUBLANES + t_tile:2 * SUBLANES + t_tile, :] = jnp.where(jt < n_t - 1, xn_ref[0], 0.0)
        left = CONV_W // 2
        xc = cb_ref[...] + xext_scr[pl.ds(SUBLANES - left, t_tile), :] * cw_ref[0:1, :]
        for k in range(1, CONV_W):
            xc = xc + xext_scr[pl.ds(SUBLANES - left + k, t_tile), :] * cw_ref[k:k + 1, :]
        pre = jnp.dot(xc.astype(BF16), w_ref[d, 0], preferred_element_type=F32)
        r_gate = jax.nn.sigmoid(pre[:, :c] + ba_ref[d:d + 1, :])
        i_gate = jax.nn.sigmoid(pre[:, c:] + bx_ref[d:d + 1, :])
        lam = lam_ref[d:d + 1, :]
        softplus_neg_lam = jnp.maximum(-lam, 0.0) + jnp.log1p(jnp.exp(-jnp.abs(lam)))
        log_a = -RG_C * r_gate * softplus_neg_lam
        a = jnp.exp(log_a)
        mult = jnp.sqrt(1.0 - a * a)
        if reset:
            first = 0 if d == 0 else seq_len - 1
            mult = jnp.where(jt * t_tile + row == first, 1.0, mult)
        b = mult * i_gate * xc
        for s in (1, 2, 4):
            if d == 0:
                a_sh = pltpu.roll(a, s, axis=0)
                b_sh = pltpu.roll(b, s, axis=0)
                m = sub >= s
            else:
                a_sh = pltpu.roll(a, t_tile - s, axis=0)
                b_sh = pltpu.roll(b, t_tile - s, axis=0)
                m = sub < SUBLANES - s
            b = jnp.where(m, b + a * b_sh, b)
            a = jnp.where(m, a * a_sh, a)
        a_scr[...] = a
        b_scr[...] = b

        def body(g, h):
            gg = g if d == 0 else n_grp - 1 - g
            off = pl.multiple_of(gg * SUBLANES, SUBLANES)
            h_rows = b_scr[pl.ds(off, SUBLANES), :] + a_scr[pl.ds(off, SUBLANES), :] * h
            y_ref[0, pl.ds(off, SUBLANES), :] = h_rows.astype(y_ref.dtype)
            return h_rows[SUBLANES - 1:SUBLANES, :] if d == 0 else h_rows[0:1, :]

        h = lax.fori_loop(0, n_grp, body, carry_scr[d:d + 1, :], unroll=4)
        carry_scr[d:d + 1, :] = h
        hl_ref[0, d:d + 1, :] = h

    direction(0, xf_ref, xfp_ref, xfn_ref, yf_ref)
    direction(1, xb_ref, xbp_ref, xbn_ref, yb_ref)


def _rglru(xr, conv_w, conv_b, w_gates, rg_ba, rg_bx, rg_lam, h0, reset):
    bsz, t, d_rnn = xr.shape
    n_blk, c = w_gates.shape[1], w_gates.shape[2]
    t_tile = _pick(t, 1024)
    n_t = t // t_tile
    tb = t_tile // SUBLANES
    last_blk = t // SUBLANES - 1

    def main(rev):
        return lambda b, n, j: (b, (n_t - 1 - j) if rev else j, n)

    def prev(rev):
        return lambda b, n, j: (b, jnp.maximum(((n_t - 1 - j) if rev else j) * tb - 1, 0), n)

    def nxt(rev):
        return lambda b, n, j: (b, jnp.minimum((((n_t - 1 - j) if rev else j) + 1) * tb, last_blk), n)

    vec = lambda rows: pl.BlockSpec((rows, c), lambda b, n, j: (0, n))
    return pl.pallas_call(
        functools.partial(_rglru_kernel, t_tile=t_tile, n_t=n_t, seq_len=t, reset=reset),
        out_shape=(
            jax.ShapeDtypeStruct((bsz, t, d_rnn), BF16),
            jax.ShapeDtypeStruct((bsz, t, d_rnn), BF16),
            jax.ShapeDtypeStruct((bsz, 2, d_rnn), F32),
        ),
        grid=(bsz, n_blk, n_t),
        in_specs=[
            pl.BlockSpec((1, t_tile, c), main(False)),
            pl.BlockSpec((1, SUBLANES, c), prev(False)),
            pl.BlockSpec((1, SUBLANES, c), nxt(False)),
            pl.BlockSpec((1, t_tile, c), main(True)),
            pl.BlockSpec((1, SUBLANES, c), prev(True)),
            pl.BlockSpec((1, SUBLANES, c), nxt(True)),
            vec(CONV_W),
            vec(1),
            pl.BlockSpec((2, 1, c, 2 * c), lambda b, n, j: (0, n, 0, 0)),
            vec(2), vec(2), vec(2),
            pl.BlockSpec((1, 2, c), lambda b, n, j: (b, 0, n)),
        ],
        out_specs=(
            pl.BlockSpec((1, t_tile, c), main(False)),
            pl.BlockSpec((1, t_tile, c), main(True)),
            pl.BlockSpec((1, 2, c), lambda b, n, j: (b, 0, n)),
        ),
        scratch_shapes=[
            pltpu.VMEM((t_tile + 2 * SUBLANES, c), F32),
            pltpu.VMEM((t_tile, c), F32),
            pltpu.VMEM((t_tile, c), F32),
            pltpu.VMEM((2, c), F32),
        ],
        compiler_params=_cparams(("arbitrary", "arbitrary", "arbitrary")),
        name="rglru",
    )(xr, xr, xr, xr, xr, xr, conv_w, conv_b.reshape(1, d_rnn), w_gates, rg_ba, rg_bx, rg_lam, h0)


def _qk(q, k):
    return lax.dot_general(q, k, (((1,), (1,)), ((), ())), preferred_element_type=F32)


def _cattn_kernel(q_ref, k_ref, v_ref, o_ref, *, n_heads, hd, scale):
    for h in range(n_heads):
        sl = slice(h * hd, (h + 1) * hd)
        q = q_ref[0, :, sl].astype(BF16)
        k = k_ref[0, :, sl].astype(BF16)
        v = v_ref[0, :, sl].astype(BF16)
        s = _qk(q, k) * scale
        p = jnp.exp(s - jnp.max(s, axis=-1, keepdims=True))
        l = jnp.sum(p, axis=-1, keepdims=True)
        o = jnp.dot(p.astype(BF16), v, preferred_element_type=F32) / l
        o_ref[0, :, sl] = o.astype(o_ref.dtype)


def _cattn(qkv, n_heads, hd):
    bsz, s, _ = qkv.shape
    d_na = n_heads * hd
    spec = lambda col: pl.BlockSpec((1, s, d_na), lambda b: (b, 0, col))
    return pl.pallas_call(
        functools.partial(_cattn_kernel, n_heads=n_heads, hd=hd, scale=hd ** -0.5),
        out_shape=jax.ShapeDtypeStruct((bsz, s, d_na), BF16),
        grid=(bsz,),
        in_specs=[spec(0), spec(1), spec(2)],
        out_specs=pl.BlockSpec((1, s, d_na), lambda b: (b, 0, 0)),
        compiler_params=_cparams(("arbitrary",)),
        name="cattn",
    )(qkv, qkv, qkv)


def _nattn_window_start(rb, rows, r_blk, w_blk):
    return jnp.clip(rb * r_blk - WIN_ROWS // 2, 0, rows - w_blk)


def _nattn_kernel(q_ref, k_ref, v_ref, ck_ref, cv_ref, bias_ref, o_ref, *, hd, scale, rows, r_blk, w_blk):
    rb = pl.program_id(1)
    w0 = _nattn_window_start(rb, rows, r_blk, w_blk)
    start = pl.multiple_of(w0 * GRID_W, GRID_W)
    k_win = k_ref[0, pl.ds(start, w_blk * GRID_W), :]
    v_win = v_ref[0, pl.ds(start, w_blk * GRID_W), :]
    q = q_ref[0]
    outs = []
    for h in range(q.shape[-1] // hd):
        sl = slice(h * hd, (h + 1) * hd)
        qh = q[:, sl]
        s_lat = _qk(qh, k_win[:, sl]) * scale + bias_ref[h, 0]
        s_ctx = _qk(qh, ck_ref[0, h].astype(BF16)) * scale
        m = jnp.maximum(jnp.max(s_lat, axis=-1, keepdims=True), jnp.max(s_ctx, axis=-1, keepdims=True))
        p_lat = jnp.exp(s_lat - m)
        p_ctx = jnp.exp(s_ctx - m)
        l = jnp.sum(p_lat, axis=-1, keepdims=True) + jnp.sum(p_ctx, axis=-1, keepdims=True)
        o = (jnp.dot(p_lat.astype(BF16), v_win[:, sl], preferred_element_type=F32)
             + jnp.dot(p_ctx.astype(BF16), cv_ref[0, h].astype(BF16), preferred_element_type=F32))
        outs.append(o / l)
    o_ref[0] = jnp.concatenate(outs, axis=-1).astype(o_ref.dtype)


def _nattn_bias(rpb, rows, r_blk, w_blk):
    n_rb = rows // r_blk
    wr = min(WIN_ROWS, rows)
    n_heads = rpb.shape[0]
    col = np.arange(GRID_W)
    cstart = np.clip(col - WIN_COLS // 2, 0, GRID_W - WIN_COLS)
    col_ok = (col[None, :] >= cstart[:, None]) & (col[None, :] < cstart[:, None] + WIN_COLS)
    rp = jnp.pad(rpb.astype(F32), ((0, 0), (0, 0), (GRID_W, GRID_W)))
    shifted = [rp[:, :, GRID_W + WIN_COLS - 1 - qc:2 * GRID_W + WIN_COLS - 1 - qc] for qc in range(GRID_W)]
    tiles = jnp.where(col_ok[None, None], jnp.stack(shifted, axis=2), NEG_INF)
    masked = jnp.full((n_heads, GRID_W, GRID_W), NEG_INF, F32)
    pats = []
    for rb in (0, min(1, n_rb - 1), n_rb - 1):
        r0 = rb * r_blk
        w0 = int(np.clip(r0 - WIN_ROWS // 2, 0, rows - w_blk))
        strips = []
        for i in range(r_blk):
            qrow = r0 + i
            rstart = int(np.clip(qrow - wr // 2, 0, rows - wr))
            strip = []
            for jj in range(w_blk):
                krow = w0 + jj
                ok = rstart <= krow < rstart + wr
                strip.append(tiles[:, krow - qrow + WIN_ROWS - 1] if ok else masked)
            strips.append(jnp.concatenate(strip, axis=-1))
        pats.append(jnp.concatenate(strips, axis=-2))
    return jnp.stack(pats, axis=1)


def _nattn(qkv, ck, cv, rpb, n_heads, hd):
    bsz, t, _ = qkv.shape
    d_na = n_heads * hd
    rows = t // GRID_W
    r_blk = min(NATTN_ROWS, rows)
    w_blk = min(r_blk + WIN_ROWS - 1, rows)
    n_rb = rows // r_blk
    for rb in range(1, n_rb - 1):
        assert 0 <= rb * r_blk - WIN_ROWS // 2 <= rows - w_blk
    hp = LANES // hd
    n_pair = d_na // LANES
    past = ck.shape[2]
    bias = _nattn_bias(rpb, rows, r_blk, w_blk)
    tq, tk = r_blk * GRID_W, w_blk * GRID_W

    def pat(rb):
        return jnp.where(rb == 0, 0, jnp.where(rb == n_rb - 1, 2, 1))

    return pl.pallas_call(
        functools.partial(_nattn_kernel, hd=hd, scale=hd ** -0.5, rows=rows, r_blk=r_blk, w_blk=w_blk),
        out_shape=jax.ShapeDtypeStruct((bsz, t, d_na), BF16),
        grid=(n_pair, n_rb, bsz),
        in_specs=[
            pl.BlockSpec((1, tq, LANES), lambda p, rb, b: (b, rb, p)),
            pl.BlockSpec((1, t, LANES), lambda p, rb, b: (b, 0, n_pair + p)),
            pl.BlockSpec((1, t, LANES), lambda p, rb, b: (b, 0, 2 * n_pair + p)),
            pl.BlockSpec((1, hp, past, hd), lambda p, rb, b: (b, p, 0, 0)),
            pl.BlockSpec((1, hp, past, hd), lambda p, rb, b: (b, p, 0, 0)),
            pl.BlockSpec((hp, 1, tq, tk), lambda p, rb, b: (p, pat(rb), 0, 0)),
        ],
        out_specs=pl.BlockSpec((1, tq, LANES), lambda p, rb, b: (b, rb, p)),
        compiler_params=_cparams(("arbitrary", "arbitrary", "arbitrary")),
        name="nattn",
    )(qkv, qkv, qkv, ck, cv, bias)


def _merge_kernel(yf_ref, yb_ref, ya_ref, ga_ref, gb_ref, wr_ref, wa_ref, o_ref):
    y_rnn = (yf_ref[...].astype(F32) + yb_ref[...].astype(F32)).astype(BF16)
    t_rnn = jnp.dot(y_rnn, wr_ref[...], preferred_element_type=F32)
    t_na = jnp.dot(ya_ref[...], wa_ref[...], preferred_element_type=F32)
    merged = (jax.nn.sigmoid(ga_ref[...].astype(F32)) * t_rnn
              + jax.nn.sigmoid(gb_ref[...].astype(F32)) * t_na)
    o_ref[...] = merged.astype(o_ref.dtype)


def _merge(yf, yb, ya, g, w_rnn_out, w_na_out):
    n_tok, d_rnn = yf.shape
    d_na = ya.shape[1]
    d = w_rnn_out.shape[1]
    tm = _pick(n_tok, 512)
    tn = _pick(d, 1024)
    n_n = d // tn
    return pl.pallas_call(
        _merge_kernel,
        out_shape=jax.ShapeDtypeStruct((n_tok, d), BF16),
        grid=(n_tok // tm, n_n),
        in_specs=[
            pl.BlockSpec((tm, d_rnn), lambda i, j: (i, 0)),
            pl.BlockSpec((tm, d_rnn), lambda i, j: (i, 0)),
            pl.BlockSpec((tm, d_na), lambda i, j: (i, 0)),
            pl.BlockSpec((tm, tn), lambda i, j: (i, j)),
            pl.BlockSpec((tm, tn), lambda i, j: (i, n_n + j)),
            pl.BlockSpec((d_rnn, tn), lambda i, j: (0, j)),
            pl.BlockSpec((d_na, tn), lambda i, j: (0, j)),
        ],
        out_specs=pl.BlockSpec((tm, tn), lambda i, j: (i, j)),
        compiler_params=_cparams(("arbitrary", "arbitrary")),
        name="merge",
    )(yf, yb, ya, g, g, w_rnn_out, w_na_out)


def _layernorm(z, g, b):
    mu = jnp.mean(z, axis=-1, keepdims=True)
    zc = z - mu
    var = jnp.mean(zc * zc, axis=-1, keepdims=True)
    return zc * lax.rsqrt(var + LN_EPS) * g + b


def _pack_bf16_pairs(x):
    half = x.shape[-1] // 2
    lo = lax.bitcast_convert_type(x[:, :half].astype(BF16).astype(F32), jnp.uint32)
    hi = lax.bitcast_convert_type(x[:, half:].astype(BF16).astype(F32), jnp.uint32)
    return (lo >> 16) | (hi & jnp.uint32(0xFFFF0000))


def _unpack_bf16_pairs(w):
    lo = lax.bitcast_convert_type(w << 16, F32).astype(BF16)
    hi = lax.bitcast_convert_type(w & jnp.uint32(0xFFFF0000), F32).astype(BF16)
    return lo, hi


def _post1_kernel(m_ref, x_ref, g1_ref, sh2_ref, sc2_ref, wo_ref, bo_ref, lng_ref, lnb_ref, wrt_ref, brt_ref,
                  x1_ref, u2_ref, ti_ref, tg_ref, *, alpha, n_experts):
    mix = jnp.dot(m_ref[...], wo_ref[...], preferred_element_type=F32) + bo_ref[...]
    x1 = _layernorm(alpha * x_ref[...] + g1_ref[0] * mix, lng_ref[...], lnb_ref[...])
    x1_ref[...] = x1
    u2 = x1 * (1.0 + sc2_ref[0]) + sh2_ref[0]
    u2_ref[...] = _pack_bf16_pairs(u2)
    hi = u2.astype(BF16)
    lo = (u2 - hi.astype(F32)).astype(BF16)
    ph = jnp.dot(hi, wrt_ref[...], preferred_element_type=F32)
    plo = jnp.dot(lo, wrt_ref[...], preferred_element_type=F32)
    logits = (ph[:, :LANES] + ph[:, LANES:]) + (plo[:, :LANES] + plo[:, LANES:]) + brt_ref[...]
    lane = lax.broadcasted_iota(jnp.int32, logits.shape, 1)
    lane_f = lane.astype(F32)
    logits = jnp.where(lane < n_experts, logits, NEG_INF)
    top_i = jnp.zeros(logits.shape, jnp.int32)
    top_e = jnp.zeros(logits.shape, F32)
    v0 = None
    denom = None
    for k in range(TOP_K):
        v = jnp.max(logits, axis=-1, keepdims=True)
        idx = jnp.min(jnp.where(logits == v, lane_f, float(LANES)), axis=-1, keepdims=True).astype(jnp.int32)
        if k == 0:
            v0 = v
        e = jnp.exp(v - v0)
        denom = e if k == 0 else denom + e
        top_i = jnp.where(lane == k, idx, top_i)
        top_e = jnp.where(lane == k, e, top_e)
        logits = jnp.where(lane == idx, NEG_INF, logits)
    ti_ref[...] = top_i
    tg_ref[...] = top_e / denom


def _post1(merged, x, mod, w_o, b_o, ln_g, ln_b, w_rt, b_rt, tok_per_mod, alpha, n_experts):
    n_tok, d = x.shape
    tm = _pick(tok_per_mod, 256)

    def mod_map(col):
        return lambda i: ((i * tm) // tok_per_mod, 0, col)

    row = lambda a: pl.BlockSpec((1, a.shape[-1]), lambda i: (0, 0))
    tile = pl.BlockSpec((tm, d), lambda i: (i, 0))
    small = pl.BlockSpec((tm, LANES), lambda i: (i, 0))
    return pl.pallas_call(
        functools.partial(_post1_kernel, alpha=alpha, n_experts=n_experts),
        out_shape=(
            jax.ShapeDtypeStruct((n_tok, d), F32),
            jax.ShapeDtypeStruct((n_tok, d // 2), jnp.uint32),
            jax.ShapeDtypeStruct((n_tok, LANES), jnp.int32),
            jax.ShapeDtypeStruct((n_tok, LANES), F32),
        ),
        grid=(n_tok // tm,),
        in_specs=[
            tile, tile,
            pl.BlockSpec((1, 1, d), mod_map(2)),
            pl.BlockSpec((1, 1, d), mod_map(3)),
            pl.BlockSpec((1, 1, d), mod_map(4)),
            pl.BlockSpec((d, d), lambda i: (0, 0)),
            row(b_o), row(ln_g), row(ln_b),
            pl.BlockSpec((d, 2 * LANES), lambda i: (0, 0)),
            row(b_rt),
        ],
        out_specs=(tile, pl.BlockSpec((tm, d // 2), lambda i: (i, 0)), small, small),
        compiler_params=_cparams(("arbitrary",)),
        name="post1",
    )(merged, x, mod, mod, mod, w_o, b_o, ln_g, ln_b, w_rt, b_rt)


def _dispatch_kernel(lpos_ref, meta_ref, x_ref, xs_hbm, stage_scr, sem, bsem, nprev_scr, *, tt, tm, n_exp, n_tiles):
    i = pl.program_id(0)
    slot = i % 2
    n_seg = n_exp + 1
    is_fill = i == n_tiles

    @pl.when(i == 0)
    def _():
        stage_scr[...] = jnp.zeros(stage_scr.shape, stage_scr.dtype)

    @pl.when(is_fill)
    def _():
        stage_scr[slot] = jnp.zeros(stage_scr.shape[1:], stage_scr.dtype)

    @pl.when(jnp.logical_not(is_fill))
    def _():
        def place(t, _):
            row = x_ref[pl.ds(t, 1), :]
            for k in range(TOP_K):
                stage_scr[slot, pl.ds(lpos_ref[0, 0, t * TOP_K + k], 1), :] = row
            return 0
        lax.fori_loop(0, tt, place, 0, unroll=2)

    def chunk_copy(s, src_row, dst_row):
        return pltpu.make_async_copy(stage_scr.at[s, pl.ds(src_row, SUBLANES), :],
                                     xs_hbm.at[pl.ds(dst_row, SUBLANES), :], sem.at[s])

    def block_copy(dst_row):
        return pltpu.make_async_copy(stage_scr.at[slot, pl.ds(0, tm), :], xs_hbm.at[pl.ds(dst_row, tm), :], bsem)

    src_step = jnp.where(is_fill, 0, SUBLANES)
    total = 0
    for e in range(n_exp):
        loff, gbase, n_chunk = meta_ref[0, 0, e], meta_ref[0, 0, n_seg + e], meta_ref[0, 0, 2 * n_seg + e]

        def start(c, _, loff=loff, gbase=gbase):
            chunk_copy(slot, pl.multiple_of(loff + c * src_step, SUBLANES),
                       pl.multiple_of(gbase + c * SUBLANES, SUBLANES)).start()
            return 0
        lax.fori_loop(0, n_chunk, start, 0)
        total = total + n_chunk

    def drain(s, n):
        def wait(c, _):
            chunk_copy(s, 0, 0).wait()
            return 0
        lax.fori_loop(0, n, wait, 0)

    @pl.when(i > 0)
    def _():
        drain(1 - slot, nprev_scr[0])
    nprev_scr[0] = total

    @pl.when(is_fill)
    def _():
        first_row, n_blk = meta_ref[0, 0, n_seg + n_exp], meta_ref[0, 0, 2 * n_seg + n_exp]

        def start(b, _):
            block_copy(pl.multiple_of(first_row + b * tm, tm)).start()
            return 0
        lax.fori_loop(0, n_blk, start, 0)

        def wait(b, _):
            block_copy(0).wait()
            return 0
        lax.fori_loop(0, n_blk, wait, 0)
        drain(slot, total)


def _dispatch(u_packed, lpos, meta, n_slots, tt, tm, n_exp):
    n_tok, dw = u_packed.shape
    n_tiles = n_tok // tt
    stage_rows = max(tt * TOP_K + n_exp * SUBLANES, tm)
    tile = lambda i: jnp.minimum(i, n_tiles - 1)
    return pl.pallas_call(
        functools.partial(_dispatch_kernel, tt=tt, tm=tm, n_exp=n_exp, n_tiles=n_tiles),
        out_shape=jax.ShapeDtypeStruct((n_slots, dw), u_packed.dtype),
        grid=(n_tiles + 1,),
        in_specs=[
            pl.BlockSpec((1, 1, lpos.shape[-1]), lambda i: (tile(i), 0, 0), memory_space=pltpu.SMEM),
            pl.BlockSpec((1, 1, meta.shape[-1]), lambda i: (i, 0, 0), memory_space=pltpu.SMEM),
            pl.BlockSpec((tt, dw), lambda i: (tile(i), 0)),
        ],
        out_specs=pl.BlockSpec(memory_space=pl.ANY),
        scratch_shapes=[
            pltpu.VMEM((2, stage_rows, dw), u_packed.dtype),
            pltpu.SemaphoreType.DMA((2,)),
            pltpu.SemaphoreType.DMA,
            pltpu.SMEM((1,), jnp.int32),
        ],
        compiler_params=_cparams(("arbitrary",)),
        name="dispatch",
    )(lpos, meta, u_packed)


def _moe_kernel(be_ref, br_ref, x_ref, wg_ref, wu_ref, wd_ref, bg_ref, bu_ref, bd_ref, o_ref, xb_scr, *, n_f):
    i = pl.program_id(0)
    f = pl.program_id(1)
    n_rows = br_ref[i]
    half = x_ref.shape[-1]

    @pl.when((n_rows > 0) & (f == 0))
    def _():
        row = lax.broadcasted_iota(jnp.int32, x_ref.shape, 0)
        lo, hi = _unpack_bf16_pairs(jnp.where(row < n_rows, x_ref[...], jnp.uint32(0)))
        xb_scr[:, :half] = lo
        xb_scr[:, half:] = hi

    @pl.when(n_rows > 0)
    def _():
        xb = xb_scr[...]
        gate = jnp.dot(xb, wg_ref[0].astype(BF16), preferred_element_type=F32) + bg_ref[0]
        up = jnp.dot(xb, wu_ref[0].astype(BF16), preferred_element_type=F32) + bu_ref[0]
        gate = jnp.minimum(gate, SWIGLU_LIMIT)
        up = jnp.clip(up, -SWIGLU_LIMIT, SWIGLU_LIMIT)
        glu = gate * jax.nn.sigmoid(SWIGLU_ALPHA * gate)
        act = ((up + 1.0) * glu).astype(BF16)
        part = jnp.dot(act, wd_ref[0].astype(BF16), preferred_element_type=F32)

        @pl.when(f == 0)
        def _():
            o_ref[...] = part + bd_ref[0]

        @pl.when(f > 0)
        def _():
            o_ref[...] += part

    @pl.when((n_rows == 0) & (f == 0))
    def _():
        o_ref[...] = jnp.zeros(o_ref.shape, o_ref.dtype)


def _moe(xs, blk_e, blk_rows, w_gu, b_gu, w_dn, b_dn, tm):
    n_slots, dw = xs.shape
    n_blocks = n_slots // tm
    n_exp, d, two_ff = w_gu.shape
    d_ff = two_ff // 2
    tf = _pick(d_ff, MOE_TF)
    n_f = d_ff // tf

    def fe(f, br, i):
        return jnp.where(br[i] > 0, f, n_f - 1)

    grid_spec = pltpu.PrefetchScalarGridSpec(
        num_scalar_prefetch=2,
        grid=(n_blocks, n_f),
        in_specs=[
            pl.BlockSpec((tm, dw), lambda i, f, be, br: (jnp.where(br[i] > 0, i, 0), 0)),
            pl.BlockSpec((1, d, tf), lambda i, f, be, br: (be[i], 0, fe(f, br, i))),
            pl.BlockSpec((1, d, tf), lambda i, f, be, br: (be[i], 0, n_f + fe(f, br, i))),
            pl.BlockSpec((1, tf, d), lambda i, f, be, br: (be[i], fe(f, br, i), 0)),
            pl.BlockSpec((1, 1, tf), lambda i, f, be, br: (be[i], 0, fe(f, br, i))),
            pl.BlockSpec((1, 1, tf), lambda i, f, be, br: (be[i], 0, n_f + fe(f, br, i))),
            pl.BlockSpec((1, 1, d), lambda i, f, be, br: (be[i], 0, 0)),
        ],
        out_specs=pl.BlockSpec((tm, d), lambda i, f, be, br: (i, 0)),
        scratch_shapes=[pltpu.VMEM((tm, d), BF16)],
    )
    return pl.pallas_call(
        functools.partial(_moe_kernel, n_f=n_f),
        out_shape=jax.ShapeDtypeStruct((n_slots, d), F32),
        grid_spec=grid_spec,
        compiler_params=_cparams(("arbitrary", "arbitrary")),
        name="moe",
    )(blk_e, blk_rows, xs, w_gu, w_gu, w_dn,
      b_gu.reshape(n_exp, 1, two_ff), b_gu.reshape(n_exp, 1, two_ff), b_dn.reshape(n_exp, 1, d))


def _route(top_i, n_exp, tt, tm):
    n_tok = top_i.shape[0]
    n_tiles = n_tok // tt
    n_a = tt * TOP_K
    n_blocks = -(-(n_tok * TOP_K + n_tiles * n_exp * (SUBLANES - 1)) // tm) + n_exp
    e = top_i.reshape(n_tiles, n_a)
    onehot = (e[:, :, None] == jnp.arange(n_exp, dtype=jnp.int32)).astype(jnp.int32)
    csum = jnp.cumsum(onehot, axis=1)
    rank = jnp.sum((csum - 1) * onehot, axis=-1)
    cnt = csum[:, -1, :]
    seg = (cnt + SUBLANES - 1) // SUBLANES * SUBLANES
    loff = jnp.cumsum(seg, axis=1) - seg
    lpos = jnp.sum(onehot * loff[:, None, :], axis=-1) + rank
    group = jnp.sum(seg, axis=0)
    group_pad = (group + tm - 1) // tm * tm
    pad_end = jnp.cumsum(group_pad)
    base = pad_end - group_pad
    gbase = base[None, :] + jnp.cumsum(seg, axis=0) - seg
    zero_col = jnp.zeros((n_tiles, 1), jnp.int32)
    tile_meta = jnp.concatenate([loff, zero_col, gbase, zero_col, seg // SUBLANES, zero_col], axis=1)
    fill_meta = jnp.concatenate([
        jnp.zeros((n_exp + 1,), jnp.int32),
        base + group, pad_end[-1:],
        (group_pad - group) // SUBLANES, n_blocks - pad_end[-1:] // tm])
    meta = jnp.concatenate([tile_meta, fill_meta[None]], axis=0).astype(jnp.int32)
    blk_start = jnp.arange(n_blocks, dtype=jnp.int32) * tm
    blk_e = jnp.minimum(jnp.sum(blk_start[:, None] >= pad_end[None, :], axis=1), n_exp - 1).astype(jnp.int32)
    blk_rows = jnp.clip(base[blk_e] + group[blk_e] - blk_start, 0, tm).astype(jnp.int32)
    return (lpos.astype(jnp.int32).reshape(n_tiles, 1, n_a), meta.reshape(n_tiles + 1, 1, 3 * (n_exp + 1)),
            blk_e, blk_rows, n_blocks)


def _post2_kernel(lpos_ref, gate_ref, meta_ref, meta_nxt_ref, ys_hbm, x_ref, g2_ref, lng_ref, lnb_ref, o_ref,
                  stage_scr, y_scr, sem, *, tt, n_exp, n_tiles, alpha):
    i = pl.program_id(0)
    slot = i % 2

    def chunk_copy(s, src_row, dst_row):
        return pltpu.make_async_copy(ys_hbm.at[pl.ds(src_row, SUBLANES), :],
                                     stage_scr.at[s, pl.ds(dst_row, SUBLANES), :], sem.at[s])

    n_seg = n_exp + 1

    def fetch(m_ref, s):
        for e in range(n_exp):
            loff, gbase, n_chunk = m_ref[0, 0, e], m_ref[0, 0, n_seg + e], m_ref[0, 0, 2 * n_seg + e]

            def start(c, _, loff=loff, gbase=gbase):
                chunk_copy(s, pl.multiple_of(gbase + c * SUBLANES, SUBLANES),
                           pl.multiple_of(loff + c * SUBLANES, SUBLANES)).start()
                return 0
            lax.fori_loop(0, n_chunk, start, 0)

    @pl.when(i == 0)
    def _():
        fetch(meta_ref, 0)

    @pl.when(i + 1 < n_tiles)
    def _():
        fetch(meta_nxt_ref, 1 - slot)

    total = meta_ref[0, 0, 2 * n_seg]
    for e in range(1, n_exp):
        total = total + meta_ref[0, 0, 2 * n_seg + e]

    def wait(c, _):
        chunk_copy(slot, 0, 0).wait()
        return 0
    lax.fori_loop(0, total, wait, 0)

    def token(t, _):
        a0 = t * TOP_K
        acc = gate_ref[0, 0, a0] * stage_scr[slot, pl.ds(lpos_ref[0, 0, a0], 1), :]
        for k in range(1, TOP_K):
            acc = acc + gate_ref[0, 0, a0 + k] * stage_scr[slot, pl.ds(lpos_ref[0, 0, a0 + k], 1), :]
        y_scr[pl.ds(t, 1), :] = acc
        return 0
    lax.fori_loop(0, tt, token, 0, unroll=2)
    o_ref[...] = _layernorm(alpha * x_ref[...] + g2_ref[0] * y_scr[...], lng_ref[...], lnb_ref[...])


def _post2(ys, lpos, gates, meta, x1, mod, ln_g, ln_b, tok_per_mod, alpha, tile_off, tt, n_exp):
    n_tok, d = x1.shape
    n_tiles = n_tok // tt
    last_tile = meta.shape[0] - 2
    stage_rows = tt * TOP_K + n_exp * SUBLANES
    cur = lambda i: (tile_off + i, 0, 0)
    nxt = lambda i: (jnp.minimum(tile_off + i + 1, last_tile), 0, 0)
    smem = lambda a, fn: pl.BlockSpec((1, 1, a.shape[-1]), fn, memory_space=pltpu.SMEM)
    row = lambda a: pl.BlockSpec((1, a.shape[-1]), lambda i: (0, 0))
    return pl.pallas_call(
        functools.partial(_post2_kernel, tt=tt, n_exp=n_exp, n_tiles=n_tiles, alpha=alpha),
        out_shape=jax.ShapeDtypeStruct((n_tok, d), F32),
        grid=(n_tiles,),
        in_specs=[
            smem(lpos, cur), smem(gates, cur), smem(meta, cur), smem(meta, nxt),
            pl.BlockSpec(memory_space=pl.ANY),
            pl.BlockSpec((tt, d), lambda i: (i, 0)),
            pl.BlockSpec((1, 1, d), lambda i: ((i * tt) // tok_per_mod, 0, 5)),
            row(ln_g), row(ln_b),
        ],
        out_specs=pl.BlockSpec((tt, d), lambda i: (i, 0)),
        scratch_shapes=[
            pltpu.VMEM((2, stage_rows, d), F32),
            pltpu.VMEM((tt, d), F32),
            pltpu.SemaphoreType.DMA((2,)),
        ],
        compiler_params=_cparams(("arbitrary",)),
        name="post2",
    )(lpos, gates, meta, meta, ys, x1, mod, ln_g, ln_b)


def _router_weights(w_router, b_router):
    d, n_exp = w_router.shape
    hi = w_router.astype(BF16)
    lo = (w_router - hi.astype(F32)).astype(BF16)
    pad = lambda a: jnp.pad(a, ((0, 0), (0, LANES - n_exp)))
    return jnp.concatenate([pad(hi), pad(lo)], axis=1), jnp.pad(b_router, (0, LANES - n_exp)).reshape(1, LANES)


def kernel(x_prompt, x_sample, cache_k, cache_v, state_h_fwd, state_h_bwd, c, c_ctx, w_ada, b_ada, w_in, b_in,
           conv_w, conv_b, rg_wa, rg_ba, rg_wx, rg_bx, rg_lam, rpb, w_rnn_out, w_na_out, w_o, b_o,
           ln1_g, ln1_b, ln2_g, ln2_b, w_router, b_router, w_gu, b_gu, w_dn, b_dn):
    depth, d, _ = w_in.shape
    bsz, seq, _ = x_prompt.shape
    dbsz, dseq, _ = x_sample.shape
    n_heads, hd = cache_k.shape[2], cache_k.shape[4]
    d_na = n_heads * hd
    d_rnn = conv_w.shape[2]
    n_exp = w_router.shape[2]
    alpha = (2 * depth) ** 0.25
    n_p, n_s = bsz * seq, dbsz * dseq

    xp = x_prompt.reshape(n_p, d)
    xs = x_sample.reshape(n_s, d)
    n_cond = -(-(1 + dbsz) // SUBLANES) * SUBLANES
    cond = jnp.concatenate([c_ctx[None], c, jnp.zeros((n_cond - 1 - dbsz, d), F32)], axis=0)
    ks, vs, hfs, hbs = [], [], [], []
    for l in range(depth):
        mod = _ada(cond, w_ada[l], b_ada[l]).reshape(n_cond, 1, 6 * d)
        mod_p, mod_s = mod[0:1], mod[1:1 + dbsz]
        w_in_l = w_in[l].astype(BF16)
        w_gates = jnp.concatenate([rg_wa[l], rg_wx[l]], axis=-1).astype(BF16)
        w_rnn_l, w_na_l, w_o_l = w_rnn_out[l].astype(BF16), w_na_out[l].astype(BF16), w_o[l].astype(BF16)
        w_rt, b_rt = _router_weights(w_router[l], b_router[l])
        row = lambda a: a[l].reshape(1, -1)

        def mixer(x, mod_g, n_b, t, h0, tok_per_mod, is_ctx):
            xr, qkv, g = _inproj(x, mod_g, w_in_l, b_in[l], d_rnn, d_na, tok_per_mod, F32 if is_ctx else BF16)
            yf, yb, h_last = _rglru(xr.reshape(n_b, t, d_rnn), conv_w[l], conv_b[l], w_gates,
                                    rg_ba[l], rg_bx[l], rg_lam[l], h0, is_ctx)
            qkv3 = qkv.reshape(n_b, t, 3 * d_na)
            if is_ctx:
                ya = _cattn(qkv3, n_heads, hd)
            else:
                ya = _nattn(qkv3, cache_k[:, l], cache_v[:, l], rpb[l], n_heads, hd)
            merged = _merge(yf.reshape(n_b * t, d_rnn), yb.reshape(n_b * t, d_rnn), ya.reshape(n_b * t, d_na),
                            g, w_rnn_l, w_na_l)
            x1, u2, ti, tg = _post1(merged, x, mod_g, w_o_l, row(b_o), row(ln1_g), row(ln1_b), w_rt, b_rt,
                                    tok_per_mod, alpha, n_exp)
            return x1, u2, ti, tg, qkv3, h_last

        x1p, u2p, tip, tgp, qkv_p, hl_p = mixer(xp, mod_p, bsz, seq, jnp.zeros((bsz, 2, d_rnn), F32), n_p, True)
        h0_s = jnp.stack([state_h_fwd[:, l], state_h_bwd[:, l]], axis=1)
        x1s, u2s, tis, tgs, _, _ = mixer(xs, mod_s, dbsz, dseq, h0_s, dseq, False)

        n_all = n_p + n_s
        tt = _pick(np.gcd(n_p, dseq), MOE_TT)
        tm = MOE_TM
        u2 = jnp.concatenate([u2p, u2s], axis=0)
        top_i = jnp.concatenate([tip[:, :TOP_K], tis[:, :TOP_K]], axis=0)
        gates = jnp.concatenate([tgp[:, :TOP_K], tgs[:, :TOP_K]], axis=0).reshape(n_all // tt, 1, tt * TOP_K)
        lpos, meta, blk_e, blk_rows, n_blocks = _route(top_i, n_exp, tt, tm)
        xsort = _dispatch(u2, lpos, meta, n_blocks * tm, tt, tm, n_exp)
        ys = _moe(xsort, blk_e, blk_rows, w_gu[l], b_gu[l], w_dn[l], b_dn[l], tm)
        xp = _post2(ys, lpos, gates, meta, x1p, mod_p, row(ln2_g), row(ln2_b), n_p, alpha, 0, tt, n_exp)
        xs = _post2(ys, lpos, gates, meta, x1s, mod_s, row(ln2_g), row(ln2_b), dseq, alpha, n_p // tt, tt, n_exp)

        heads = lambda a: a.reshape(bsz, seq, n_heads, hd).transpose(0, 2, 1, 3)
        ks.append(heads(qkv_p[:, :, d_na:2 * d_na]))
        vs.append(heads(qkv_p[:, :, 2 * d_na:]))
        hfs.append(hl_p[:, 0])
        hbs.append(hl_p[:, 1])

    return (xp.reshape(bsz, seq, d), xs.reshape(dbsz, dseq, d),
            jnp.stack(ks, axis=1), jnp.stack(vs, axis=1), jnp.stack(hfs, axis=1), jnp.stack(hbs, axis=1))
```

```python
import functools

import numpy as np
import jax
import jax.numpy as jnp
from jax import lax
from jax.experimental import pallas as pl
from jax.experimental.pallas import tpu as pltpu

F32 = jnp.float32
BF16 = jnp.bfloat16

GRID_W = 64
WIN_ROWS = 8
WIN_COLS = 16
CONV_W = 4
RG_C = 8.0
TOP_K = 4
SWIGLU_LIMIT = 7.0
SWIGLU_ALPHA = 1.702
LN_EPS = 1e-5
NEG_INF = -1e30

VMEM_LIMIT_BYTES = 56 * 1024 * 1024
SUBLANES = 8
LANES = 128

NATTN_ROWS = 8
MOE_TT = 256
MOE_TM = 1024
MOE_TF = 256
MOE_ROW_CHUNKS = 2


def _cparams(sem):
    return pltpu.CompilerParams(dimension_semantics=sem, vmem_limit_bytes=VMEM_LIMIT_BYTES)


def _pick(n, pref):
    t = min(n, pref)
    while n % t:
        t //= 2
    return t


def _ada_kernel(c_ref, w_ref, b_ref, o_ref):
    c = c_ref[...]
    s = c * jax.nn.sigmoid(c)
    o_ref[...] = jnp.dot(s.astype(BF16), w_ref[...].astype(BF16), preferred_element_type=F32) + b_ref[...]


def _ada(cond, w_ada, b_ada):
    nb, d = cond.shape
    n = w_ada.shape[1]
    tn = _pick(n, 1024)
    return pl.pallas_call(
        _ada_kernel,
        out_shape=jax.ShapeDtypeStruct((nb, n), F32),
        grid=(n // tn,),
        in_specs=[
            pl.BlockSpec((nb, d), lambda j: (0, 0)),
            pl.BlockSpec((d, tn), lambda j: (0, j)),
            pl.BlockSpec((1, tn), lambda j: (0, j)),
        ],
        out_specs=pl.BlockSpec((nb, tn), lambda j: (0, j)),
        compiler_params=_cparams(("arbitrary",)),
        name="ada",
    )(cond, w_ada, b_ada.reshape(1, n))


def _inproj_kernel(x_ref, sh_ref, sc_ref, w_ref, b_ref, xr_ref, qkv_ref, g_ref, u_scr, *, n_xr, n_qkv):
    n = pl.program_id(1)

    @pl.when(n == 0)
    def _():
        u = x_ref[...] * (1.0 + sc_ref[0]) + sh_ref[0]
        u_scr[...] = u.astype(BF16)

    acc = jnp.dot(u_scr[...], w_ref[...], preferred_element_type=F32) + b_ref[...]

    @pl.when(n < n_xr)
    def _():
        xr_ref[...] = acc.astype(xr_ref.dtype)

    @pl.when((n >= n_xr) & (n < n_xr + n_qkv))
    def _():
        qkv_ref[...] = acc.astype(qkv_ref.dtype)

    @pl.when(n >= n_xr + n_qkv)
    def _():
        g_ref[...] = acc.astype(g_ref.dtype)


def _inproj(x, mod, w_in, b_in, d_rnn, d_na, tok_per_mod, qkv_dtype):
    n_tok, d = x.shape
    d_in = w_in.shape[1]
    tn = _pick(np.gcd(np.gcd(d_rnn, d_na), d), 1024)
    tm = _pick(tok_per_mod, 512)
    n_xr, n_qkv, n_g = d_rnn // tn, 3 * d_na // tn, 2 * d // tn
    assert (n_xr + n_qkv + n_g) * tn == d_in

    def mod_map(col):
        return lambda i, j: ((i * tm) // tok_per_mod, 0, col)

    return pl.pallas_call(
        functools.partial(_inproj_kernel, n_xr=n_xr, n_qkv=n_qkv),
        out_shape=(
            jax.ShapeDtypeStruct((n_tok, d_rnn), F32),
            jax.ShapeDtypeStruct((n_tok, 3 * d_na), qkv_dtype),
            jax.ShapeDtypeStruct((n_tok, 2 * d), BF16),
        ),
        grid=(n_tok // tm, d_in // tn),
        in_specs=[
            pl.BlockSpec((tm, d), lambda i, j: (i, 0)),
            pl.BlockSpec((1, 1, d), mod_map(0)),
            pl.BlockSpec((1, 1, d), mod_map(1)),
            pl.BlockSpec((d, tn), lambda i, j: (0, j)),
            pl.BlockSpec((1, tn), lambda i, j: (0, j)),
        ],
        out_specs=(
            pl.BlockSpec((tm, tn), lambda i, j: (i, jnp.minimum(j, n_xr - 1))),
            pl.BlockSpec((tm, tn), lambda i, j: (i, jnp.clip(j - n_xr, 0, n_qkv - 1))),
            pl.BlockSpec((tm, tn), lambda i, j: (i, jnp.clip(j - n_xr - n_qkv, 0, n_g - 1))),
        ),
        scratch_shapes=[pltpu.VMEM((tm, d), BF16)],
        compiler_params=_cparams(("arbitrary", "arbitrary")),
        name="inproj",
    )(x, mod, mod, w_in, b_in.reshape(1, d_in))


def _rglru_kernel(xf_ref, xfp_ref, xfn_ref, xb_ref, xbp_ref, xbn_ref,
                  cw_ref, cb_ref, w_ref, ba_ref, bx_ref, lam_ref, h0_ref,
                  yf_ref, yb_ref, hl_ref,
                  xext_scr, a_scr, b_scr, carry_scr, *, t_tile, n_t, seq_len, reset):
    j = pl.program_id(2)
    c = xf_ref.shape[-1]
    n_grp = t_tile // SUBLANES

    @pl.when(j == 0)
    def _():
        carry_scr[...] = h0_ref[0]

    row = lax.broadcasted_iota(jnp.int32, (t_tile, 1), 0)
    sub3 = lax.broadcasted_iota(jnp.int32, (1, SUBLANES, 1), 1)

    def direction(d, x_ref, xp_ref, xn_ref, y_ref):
        jt = j if d == 0 else n_t - 1 - j
        xext_scr[0:SUBLANES, :] = jnp.where(jt > 0, xp_ref[0], 0.0)
        xext_scr[SUBLANES:SUBLANES + t_tile, :] = x_ref[0]
        xext_scr[SUBLANES + t_tile:2 * SUBLANES + t_tile, :] = jnp.where(jt < n_t - 1, xn_ref[0], 0.0)
        left = CONV_W // 2
        xc = cb_ref[...] + xext_scr[pl.ds(SUBLANES - left, t_tile), :] * cw_ref[0:1, :]
        for k in range(1, CONV_W):
            xc = xc + xext_scr[pl.ds(SUBLANES - left + k, t_tile), :] * cw_ref[k:k + 1, :]
        pre = jnp.dot(xc.astype(BF16), w_ref[d, 0], preferred_element_type=F32)
        r_gate = jax.nn.sigmoid(pre[:, :c] + ba_ref[d:d + 1, :])
        i_gate = jax.nn.sigmoid(pre[:, c:] + bx_ref[d:d + 1, :])
        lam = lam_ref[d:d + 1, :]
        softplus_neg_lam = jnp.maximum(-lam, 0.0) + jnp.log1p(jnp.exp(-jnp.abs(lam)))
        log_a = -RG_C * r_gate * softplus_neg_lam
        a = jnp.exp(log_a)
        om = 1.0 - a * a
        mult = jnp.where(om > 0.0, om * lax.rsqrt(om), 0.0)
        if reset:
            first = 0 if d == 0 else seq_len - 1
            mult = jnp.where(jt * t_tile + row == first, 1.0, mult)
        b = mult * i_gate * xc
        a = a.reshape(n_grp, SUBLANES, c)
        b = b.reshape(n_grp, SUBLANES, c)
        for s in (1, 2, 4):
            if d == 0:
                a_sh = pltpu.roll(a, s, axis=1)
                b_sh = pltpu.roll(b, s, axis=1)
                m = sub3 >= s
            else:
                a_sh = pltpu.roll(a, SUBLANES - s, axis=1)
                b_sh = pltpu.roll(b, SUBLANES - s, axis=1)
                m = sub3 < SUBLANES - s
            b = jnp.where(m, b + a * b_sh, b)
            a = jnp.where(m, a * a_sh, a)
        a_scr[...] = a.reshape(t_tile, c)
        b_scr[...] = b.reshape(t_tile, c)

        def body(g, h):
            gg = g if d == 0 else n_grp - 1 - g
            off = pl.multiple_of(gg * SUBLANES, SUBLANES)
            h_rows = b_scr[pl.ds(off, SUBLANES), :] + a_scr[pl.ds(off, SUBLANES), :] * h
            y_ref[0, pl.ds(off, SUBLANES), :] = h_rows.astype(y_ref.dtype)
            return h_rows[SUBLANES - 1:SUBLANES, :] if d == 0 else h_rows[0:1, :]

        h = lax.fori_loop(0, n_grp, body, carry_scr[d:d + 1, :], unroll=4)
        carry_scr[d:d + 1, :] = h
        hl_ref[0, d:d + 1, :] = h

    direction(0, xf_ref, xfp_ref, xfn_ref, yf_ref)
    direction(1, xb_ref, xbp_ref, xbn_ref, yb_ref)


def _rglru(xr, conv_w, conv_b, w_gates, rg_ba, rg_bx, rg_lam, h0, reset):
    bsz, t, d_rnn = xr.shape
    n_blk, c = w_gates.shape[1], w_gates.shape[2]
    t_tile = _pick(t, 1024)
    n_t = t // t_tile
    tb = t_tile // SUBLANES
    last_blk = t // SUBLANES - 1

    def main(rev):
        return lambda b, n, j: (b, (n_t - 1 - j) if rev else j, n)

    def prev(rev):
        return lambda b, n, j: (b, jnp.maximum(((n_t - 1 - j) if rev else j) * tb - 1, 0), n)

    def nxt(rev):
        return lambda b, n, j: (b, jnp.minimum((((n_t - 1 - j) if rev else j) + 1) * tb, last_blk), n)

    vec = lambda rows: pl.BlockSpec((rows, c), lambda b, n, j: (0, n))
    return pl.pallas_call(
        functools.partial(_rglru_kernel, t_tile=t_tile, n_t=n_t, seq_len=t, reset=reset),
        out_shape=(
            jax.ShapeDtypeStruct((bsz, t, d_rnn), BF16),
            jax.ShapeDtypeStruct((bsz, t, d_rnn), BF16),
            jax.ShapeDtypeStruct((bsz, 2, d_rnn), F32),
        ),
        grid=(bsz, n_blk, n_t),
        in_specs=[
            pl.BlockSpec((1, t_tile, c), main(False)),
            pl.BlockSpec((1, SUBLANES, c), prev(False)),
            pl.BlockSpec((1, SUBLANES, c), nxt(False)),
            pl.BlockSpec((1, t_tile, c), main(True)),
            pl.BlockSpec((1, SUBLANES, c), prev(True)),
            pl.BlockSpec((1, SUBLANES, c), nxt(True)),
            vec(CONV_W),
            vec(1),
            pl.BlockSpec((2, 1, c, 2 * c), lambda b, n, j: (0, n, 0, 0)),
            vec(2), vec(2), vec(2),
            pl.BlockSpec((1, 2, c), lambda b, n, j: (b, 0, n)),
        ],
        out_specs=(
            pl.BlockSpec((1, t_tile, c), main(False)),
            pl.BlockSpec((1, t_tile, c), main(True)),
            pl.BlockSpec((1, 2, c), lambda b, n, j: (b, 0, n)),
        ),
        scratch_shapes=[
            pltpu.VMEM((t_tile + 2 * SUBLANES, c), F32),
            pltpu.VMEM((t_tile, c), F32),
            pltpu.VMEM((t_tile, c), F32),
            pltpu.VMEM((2, c), F32),
        ],
        compiler_params=_cparams(("arbitrary", "arbitrary", "arbitrary")),
        name="rglru",
    )(xr, xr, xr, xr, xr, xr, conv_w, conv_b.reshape(1, d_rnn), w_gates, rg_ba, rg_bx, rg_lam, h0)


def _qk(q, k):
    return lax.dot_general(q, k, (((1,), (1,)), ((), ())), preferred_element_type=F32)


def _cattn_kernel(q_ref, k_ref, v_ref, o_ref, *, n_heads, hd, scale):
    for h in range(n_heads):
        sl = slice(h * hd, (h + 1) * hd)
        q = q_ref[0, :, sl].astype(BF16)
        k = k_ref[0, :, sl].astype(BF16)
        v = v_ref[0, :, sl].astype(BF16)
        s = _qk(q, k) * scale
        p = jnp.exp(s - jnp.max(s, axis=-1, keepdims=True))
        l = jnp.sum(p, axis=-1, keepdims=True)
        o = jnp.dot(p.astype(BF16), v, preferred_element_type=F32) / l
        o_ref[0, :, sl] = o.astype(o_ref.dtype)


def _cattn(qkv, n_heads, hd):
    bsz, s, _ = qkv.shape
    d_na = n_heads * hd
    spec = lambda col: pl.BlockSpec((1, s, d_na), lambda b: (b, 0, col))
    return pl.pallas_call(
        functools.partial(_cattn_kernel, n_heads=n_heads, hd=hd, scale=hd ** -0.5),
        out_shape=jax.ShapeDtypeStruct((bsz, s, d_na), BF16),
        grid=(bsz,),
        in_specs=[spec(0), spec(1), spec(2)],
        out_specs=pl.BlockSpec((1, s, d_na), lambda b: (b, 0, 0)),
        compiler_params=_cparams(("arbitrary",)),
        name="cattn",
    )(qkv, qkv, qkv)


def _nattn_window_start(rb, rows, r_blk, w_blk):
    return jnp.clip(rb * r_blk - WIN_ROWS // 2, 0, rows - w_blk)


def _nattn_kernel(q_ref, k_ref, v_ref, ck_ref, cv_ref, bias_ref, o_ref, *, hd, scale, rows, r_blk, w_blk):
    rb = pl.program_id(1)
    w0 = _nattn_window_start(rb, rows, r_blk, w_blk)
    start = pl.multiple_of(w0 * GRID_W, GRID_W)
    k_win = k_ref[0, pl.ds(start, w_blk * GRID_W), :]
    v_win = v_ref[0, pl.ds(start, w_blk * GRID_W), :]
    q = q_ref[0]
    outs = []
    for h in range(q.shape[-1] // hd):
        sl = slice(h * hd, (h + 1) * hd)
        qh = q[:, sl]
        s_lat = _qk(qh, k_win[:, sl]) * scale + bias_ref[h, 0]
        s_ctx = _qk(qh, ck_ref[0, h].astype(BF16)) * scale
        m = jnp.maximum(jnp.max(s_lat, axis=-1, keepdims=True), jnp.max(s_ctx, axis=-1, keepdims=True))
        p_lat = jnp.exp(s_lat - m)
        p_ctx = jnp.exp(s_ctx - m)
        l = jnp.sum(p_lat, axis=-1, keepdims=True) + jnp.sum(p_ctx, axis=-1, keepdims=True)
        o = (jnp.dot(p_lat.astype(BF16), v_win[:, sl], preferred_element_type=F32)
             + jnp.dot(p_ctx.astype(BF16), cv_ref[0, h].astype(BF16), preferred_element_type=F32))
        outs.append(o / l)
    o_ref[0] = jnp.concatenate(outs, axis=-1).astype(o_ref.dtype)


def _nattn_bias(rpb, rows, r_blk, w_blk):
    n_rb = rows // r_blk
    wr = min(WIN_ROWS, rows)
    n_heads = rpb.shape[0]
    col = np.arange(GRID_W)
    cstart = np.clip(col - WIN_COLS // 2, 0, GRID_W - WIN_COLS)
    col_ok = (col[None, :] >= cstart[:, None]) & (col[None, :] < cstart[:, None] + WIN_COLS)
    rp = jnp.pad(rpb.astype(F32), ((0, 0), (0, 0), (GRID_W, GRID_W)))
    shifted = [rp[:, :, GRID_W + WIN_COLS - 1 - qc:2 * GRID_W + WIN_COLS - 1 - qc] for qc in range(GRID_W)]
    tiles = jnp.where(col_ok[None, None], jnp.stack(shifted, axis=2), NEG_INF)
    masked = jnp.full((n_heads, GRID_W, GRID_W), NEG_INF, F32)
    pats = []
    for rb in (0, min(1, n_rb - 1), n_rb - 1):
        r0 = rb * r_blk
        w0 = int(np.clip(r0 - WIN_ROWS // 2, 0, rows - w_blk))
        strips = []
        for i in range(r_blk):
            qrow = r0 + i
            rstart = int(np.clip(qrow - wr // 2, 0, rows - wr))
            strip = []
            for jj in range(w_blk):
                krow = w0 + jj
                ok = rstart <= krow < rstart + wr
                strip.append(tiles[:, krow - qrow + WIN_ROWS - 1] if ok else masked)
            strips.append(jnp.concatenate(strip, axis=-1))
        pats.append(jnp.concatenate(strips, axis=-2))
    return jnp.stack(pats, axis=1)


def _nattn(qkv, ck, cv, rpb, n_heads, hd):
    bsz, t, _ = qkv.shape
    d_na = n_heads * hd
    rows = t // GRID_W
    r_blk = min(NATTN_ROWS, rows)
    w_blk = min(r_blk + WIN_ROWS - 1, rows)
    n_rb = rows // r_blk
    for rb in range(1, n_rb - 1):
        assert 0 <= rb * r_blk - WIN_ROWS // 2 <= rows - w_blk
    hp = LANES // hd
    n_pair = d_na // LANES
    past = ck.shape[2]
    bias = _nattn_bias(rpb, rows, r_blk, w_blk)
    tq, tk = r_blk * GRID_W, w_blk * GRID_W

    def pat(rb):
        return jnp.where(rb == 0, 0, jnp.where(rb == n_rb - 1, 2, 1))

    return pl.pallas_call(
        functools.partial(_nattn_kernel, hd=hd, scale=hd ** -0.5, rows=rows, r_blk=r_blk, w_blk=w_blk),
        out_shape=jax.ShapeDtypeStruct((bsz, t, d_na), BF16),
        grid=(n_pair, n_rb, bsz),
        in_specs=[
            pl.BlockSpec((1, tq, LANES), lambda p, rb, b: (b, rb, p)),
            pl.BlockSpec((1, t, LANES), lambda p, rb, b: (b, 0, n_pair + p)),
            pl.BlockSpec((1, t, LANES), lambda p, rb, b: (b, 0, 2 * n_pair + p)),
            pl.BlockSpec((1, hp, past, hd), lambda p, rb, b: (b, p, 0, 0)),
            pl.BlockSpec((1, hp, past, hd), lambda p, rb, b: (b, p, 0, 0)),
            pl.BlockSpec((hp, 1, tq, tk), lambda p, rb, b: (p, pat(rb), 0, 0)),
        ],
        out_specs=pl.BlockSpec((1, tq, LANES), lambda p, rb, b: (b, rb, p)),
        compiler_params=_cparams(("arbitrary", "arbitrary", "arbitrary")),
        name="nattn",
    )(qkv, qkv, qkv, ck, cv, bias)


def _merge_kernel(yf_ref, yb_ref, ya_ref, ga_ref, gb_ref, wr_ref, wa_ref, o_ref):
    y_rnn = (yf_ref[...].astype(F32) + yb_ref[...].astype(F32)).astype(BF16)
    t_rnn = jnp.dot(y_rnn, wr_ref[...], preferred_element_type=F32)
    t_na = jnp.dot(ya_ref[...], wa_ref[...], preferred_element_type=F32)
    merged = (jax.nn.sigmoid(ga_ref[...].astype(F32)) * t_rnn
              + jax.nn.sigmoid(gb_ref[...].astype(F32)) * t_na)
    o_ref[...] = merged.astype(o_ref.dtype)


def _merge(yf, yb, ya, g, w_rnn_out, w_na_out):
    n_tok, d_rnn = yf.shape
    d_na = ya.shape[1]
    d = w_rnn_out.shape[1]
    tm = _pick(n_tok, 512)
    tn = _pick(d, 1024)
    n_n = d // tn
    return pl.pallas_call(
        _merge_kernel,
        out_shape=jax.ShapeDtypeStruct((n_tok, d), BF16),
        grid=(n_tok // tm, n_n),
        in_specs=[
            pl.BlockSpec((tm, d_rnn), lambda i, j: (i, 0)),
            pl.BlockSpec((tm, d_rnn), lambda i, j: (i, 0)),
            pl.BlockSpec((tm, d_na), lambda i, j: (i, 0)),
            pl.BlockSpec((tm, tn), lambda i, j: (i, j)),
            pl.BlockSpec((tm, tn), lambda i, j: (i, n_n + j)),
            pl.BlockSpec((d_rnn, tn), lambda i, j: (0, j)),
            pl.BlockSpec((d_na, tn), lambda i, j: (0, j)),
        ],
        out_specs=pl.BlockSpec((tm, tn), lambda i, j: (i, j)),
        compiler_params=_cparams(("arbitrary", "arbitrary")),
        name="merge",
    )(yf, yb, ya, g, g, w_rnn_out, w_na_out)


def _layernorm(z, g, b):
    mu = jnp.mean(z, axis=-1, keepdims=True)
    zc = z - mu
    var = jnp.mean(zc * zc, axis=-1, keepdims=True)
    return zc * lax.rsqrt(var + LN_EPS) * g + b


def _pack_bf16_pairs(x):
    half = x.shape[-1] // 2
    lo = lax.bitcast_convert_type(x[:, :half].astype(BF16).astype(F32), jnp.uint32)
    hi = lax.bitcast_convert_type(x[:, half:].astype(BF16).astype(F32), jnp.uint32)
    return (lo >> 16) | (hi & jnp.uint32(0xFFFF0000))


def _unpack_bf16_pairs(w):
    lo = lax.bitcast_convert_type(w << 16, F32).astype(BF16)
    hi = lax.bitcast_convert_type(w & jnp.uint32(0xFFFF0000), F32).astype(BF16)
    return lo, hi


def _post1_kernel(m_ref, x_ref, g1_ref, sh2_ref, sc2_ref, wo_ref, bo_ref, lng_ref, lnb_ref, wrt_ref, brt_ref,
                  x1_ref, u2_ref, ti_ref, tg_ref, *, alpha, n_experts):
    mix = jnp.dot(m_ref[...], wo_ref[...], preferred_element_type=F32) + bo_ref[...]
    x1 = _layernorm(alpha * x_ref[...] + g1_ref[0] * mix, lng_ref[...], lnb_ref[...])
    x1_ref[...] = x1
    u2 = x1 * (1.0 + sc2_ref[0]) + sh2_ref[0]
    u2_ref[...] = _pack_bf16_pairs(u2)
    hi = u2.astype(BF16)
    lo = (u2 - hi.astype(F32)).astype(BF16)
    ph = jnp.dot(hi, wrt_ref[...], preferred_element_type=F32)
    plo = jnp.dot(lo, wrt_ref[...], preferred_element_type=F32)
    logits = (ph[:, :LANES] + ph[:, LANES:]) + (plo[:, :LANES] + plo[:, LANES:]) + brt_ref[...]
    lane = lax.broadcasted_iota(jnp.int32, logits.shape, 1)
    lane_f = lane.astype(F32)
    logits = jnp.where(lane < n_experts, logits, NEG_INF)
    top_i = jnp.zeros(logits.shape, jnp.int32)
    top_e = jnp.zeros(logits.shape, F32)
    v0 = None
    denom = None
    for k in range(TOP_K):
        v = jnp.max(logits, axis=-1, keepdims=True)
        idx = jnp.min(jnp.where(logits == v, lane_f, float(LANES)), axis=-1, keepdims=True).astype(jnp.int32)
        if k == 0:
            v0 = v
        e = jnp.exp(v - v0)
        denom = e if k == 0 else denom + e
        top_i = jnp.where(lane == k, idx, top_i)
        top_e = jnp.where(lane == k, e, top_e)
        logits = jnp.where(lane == idx, NEG_INF, logits)
    ti_ref[...] = top_i
    tg_ref[...] = top_e / denom


def _post1(merged, x, mod, w_o, b_o, ln_g, ln_b, w_rt, b_rt, tok_per_mod, alpha, n_experts):
    n_tok, d = x.shape
    tm = _pick(tok_per_mod, 256)

    def mod_map(col):
        return lambda i: ((i * tm) // tok_per_mod, 0, col)

    row = lambda a: pl.BlockSpec((1, a.shape[-1]), lambda i: (0, 0))
    tile = pl.BlockSpec((tm, d), lambda i: (i, 0))
    small = pl.BlockSpec((tm, LANES), lambda i: (i, 0))
    return pl.pallas_call(
        functools.partial(_post1_kernel, alpha=alpha, n_experts=n_experts),
        out_shape=(
            jax.ShapeDtypeStruct((n_tok, d), F32),
            jax.ShapeDtypeStruct((n_tok, d // 2), jnp.uint32),
            jax.ShapeDtypeStruct((n_tok, LANES), jnp.int32),
            jax.ShapeDtypeStruct((n_tok, LANES), F32),
        ),
        grid=(n_tok // tm,),
        in_specs=[
            tile, tile,
            pl.BlockSpec((1, 1, d), mod_map(2)),
            pl.BlockSpec((1, 1, d), mod_map(3)),
            pl.BlockSpec((1, 1, d), mod_map(4)),
            pl.BlockSpec((d, d), lambda i: (0, 0)),
            row(b_o), row(ln_g), row(ln_b),
            pl.BlockSpec((d, 2 * LANES), lambda i: (0, 0)),
            row(b_rt),
        ],
        out_specs=(tile, pl.BlockSpec((tm, d // 2), lambda i: (i, 0)), small, small),
        compiler_params=_cparams(("arbitrary",)),
        name="post1",
    )(merged, x, mod, mod, mod, w_o, b_o, ln_g, ln_b, w_rt, b_rt)


def _dispatch_kernel(lpos_ref, meta_ref, x_ref, xs_hbm, stage_scr, sem, bsem, nprev_scr, *, tt, tm, n_exp, n_tiles):
    i = pl.program_id(0)
    slot = i % 2
    n_seg = n_exp + 1
    is_fill = i == n_tiles

    @pl.when(i == 0)
    def _():
        stage_scr[...] = jnp.zeros(stage_scr.shape, stage_scr.dtype)

    @pl.when(is_fill)
    def _():
        stage_scr[slot] = jnp.zeros(stage_scr.shape[1:], stage_scr.dtype)

    @pl.when(jnp.logical_not(is_fill))
    def _():
        def place(t, _):
            row = x_ref[pl.ds(t, 1), :]
            for k in range(TOP_K):
                stage_scr[slot, pl.ds(lpos_ref[0, 0, t * TOP_K + k], 1), :] = row
            return 0
        lax.fori_loop(0, tt, place, 0, unroll=2)

    def chunk_copy(s, src_row, dst_row):
        return pltpu.make_async_copy(stage_scr.at[s, pl.ds(src_row, SUBLANES), :],
                                     xs_hbm.at[pl.ds(dst_row, SUBLANES), :], sem.at[s])

    def block_copy(dst_row):
        return pltpu.make_async_copy(stage_scr.at[slot, pl.ds(0, tm), :], xs_hbm.at[pl.ds(dst_row, tm), :], bsem)

    src_step = jnp.where(is_fill, 0, SUBLANES)
    total = 0
    for e in range(n_exp):
        loff, gbase, n_chunk = meta_ref[0, 0, e], meta_ref[0, 0, n_seg + e], meta_ref[0, 0, 2 * n_seg + e]

        def start(c, _, loff=loff, gbase=gbase):
            chunk_copy(slot, pl.multiple_of(loff + c * src_step, SUBLANES),
                       pl.multiple_of(gbase + c * SUBLANES, SUBLANES)).start()
            return 0
        lax.fori_loop(0, n_chunk, start, 0)
        total = total + n_chunk

    def drain(s, n):
        def wait(c, _):
            chunk_copy(s, 0, 0).wait()
            return 0
        lax.fori_loop(0, n, wait, 0)

    @pl.when(i > 0)
    def _():
        drain(1 - slot, nprev_scr[0])
    nprev_scr[0] = total

    @pl.when(is_fill)
    def _():
        first_row, n_blk = meta_ref[0, 0, n_seg + n_exp], meta_ref[0, 0, 2 * n_seg + n_exp]

        def start(b, _):
            block_copy(pl.multiple_of(first_row + b * tm, tm)).start()
            return 0
        lax.fori_loop(0, n_blk, start, 0)

        def wait(b, _):
            block_copy(0).wait()
            return 0
        lax.fori_loop(0, n_blk, wait, 0)
        drain(slot, total)


def _dispatch(u_packed, lpos, meta, n_slots, tt, tm, n_exp):
    n_tok, dw = u_packed.shape
    n_tiles = n_tok // tt
    stage_rows = max(tt * TOP_K + n_exp * SUBLANES, tm)
    tile = lambda i: jnp.minimum(i, n_tiles - 1)
    return pl.pallas_call(
        functools.partial(_dispatch_kernel, tt=tt, tm=tm, n_exp=n_exp, n_tiles=n_tiles),
        out_shape=jax.ShapeDtypeStruct((n_slots, dw), u_packed.dtype),
        grid=(n_tiles + 1,),
        in_specs=[
            pl.BlockSpec((1, 1, lpos.shape[-1]), lambda i: (tile(i), 0, 0), memory_space=pltpu.SMEM),
            pl.BlockSpec((1, 1, meta.shape[-1]), lambda i: (i, 0, 0), memory_space=pltpu.SMEM),
            pl.BlockSpec((tt, dw), lambda i: (tile(i), 0)),
        ],
        out_specs=pl.BlockSpec(memory_space=pl.ANY),
        scratch_shapes=[
            pltpu.VMEM((2, stage_rows, dw), u_packed.dtype),
            pltpu.SemaphoreType.DMA((2,)),
            pltpu.SemaphoreType.DMA,
            pltpu.SMEM((1,), jnp.int32),
        ],
        compiler_params=_cparams(("arbitrary",)),
        name="dispatch",
    )(lpos, meta, u_packed)


def _moe_kernel(be_ref, br_ref, x_ref, wg_ref, wu_ref, wd_ref, bg_ref, bu_ref, bd_ref, o_ref, xb_scr, *, n_f):
    i = pl.program_id(0)
    f = pl.program_id(1)
    n_rows = br_ref[i]
    half = x_ref.shape[-1]

    @pl.when((n_rows > 0) & (f == 0))
    def _():
        lo, hi = _unpack_bf16_pairs(x_ref[...])
        xb_scr[:, :half] = lo
        xb_scr[:, half:] = hi
        o_ref[...] = jnp.broadcast_to(bd_ref[0], o_ref.shape)

    @pl.when(n_rows > 0)
    def _():
        wg = wg_ref[0].astype(BF16)
        wu = wu_ref[0].astype(BF16)
        wd = wd_ref[0].astype(BF16)
        rows = x_ref.shape[0] // MOE_ROW_CHUNKS
        for c in range(MOE_ROW_CHUNKS):
            sl = pl.ds(c * rows, rows)
            xb = xb_scr[sl, :]
            gate = jnp.dot(xb, wg, preferred_element_type=F32) + bg_ref[0]
            up = jnp.dot(xb, wu, preferred_element_type=F32) + bu_ref[0]
            gate = jnp.minimum(gate, SWIGLU_LIMIT)
            up = jnp.clip(up, -SWIGLU_LIMIT, SWIGLU_LIMIT)
            glu = gate * jax.nn.sigmoid(SWIGLU_ALPHA * gate)
            act = ((up + 1.0) * glu).astype(BF16)
            o_ref[sl, :] += jnp.dot(act, wd, preferred_element_type=F32)

    @pl.when((n_rows == 0) & (f == 0))
    def _():
        o_ref[...] = jnp.zeros(o_ref.shape, o_ref.dtype)


def _moe(xs, blk_e, blk_rows, w_gu, b_gu, w_dn, b_dn, tm):
    n_slots, dw = xs.shape
    n_blocks = n_slots // tm
    n_exp, d, two_ff = w_gu.shape
    d_ff = two_ff // 2
    tf = _pick(d_ff, MOE_TF)
    n_f = d_ff // tf

    def fe(f, br, i):
        return jnp.where(br[i] > 0, f, n_f - 1)

    grid_spec = pltpu.PrefetchScalarGridSpec(
        num_scalar_prefetch=2,
        grid=(n_blocks, n_f),
        in_specs=[
            pl.BlockSpec((tm, dw), lambda i, f, be, br: (jnp.where(br[i] > 0, i, 0), 0)),
            pl.BlockSpec((1, d, tf), lambda i, f, be, br: (be[i], 0, fe(f, br, i))),
            pl.BlockSpec((1, d, tf), lambda i, f, be, br: (be[i], 0, n_f + fe(f, br, i))),
            pl.BlockSpec((1, tf, d), lambda i, f, be, br: (be[i], fe(f, br, i), 0)),
            pl.BlockSpec((1, 1, tf), lambda i, f, be, br: (be[i], 0, fe(f, br, i))),
            pl.BlockSpec((1, 1, tf), lambda i, f, be, br: (be[i], 0, n_f + fe(f, br, i))),
            pl.BlockSpec((1, 1, d), lambda i, f, be, br: (be[i], 0, 0)),
        ],
        out_specs=pl.BlockSpec((tm, d), lambda i, f, be, br: (i, 0)),
        scratch_shapes=[pltpu.VMEM((tm, d), BF16)],
    )
    return pl.pallas_call(
        functools.partial(_moe_kernel, n_f=n_f),
        out_shape=jax.ShapeDtypeStruct((n_slots, d), F32),
        grid_spec=grid_spec,
        compiler_params=_cparams(("arbitrary", "arbitrary")),
        name="moe",
    )(blk_e, blk_rows, xs, w_gu, w_gu, w_dn,
      b_gu.reshape(n_exp, 1, two_ff), b_gu.reshape(n_exp, 1, two_ff), b_dn.reshape(n_exp, 1, d))


def _route(top_i, n_exp, tt, tm):
    n_tok = top_i.shape[0]
    n_tiles = n_tok // tt
    n_a = tt * TOP_K
    n_blocks = -(-(n_tok * TOP_K + n_tiles * n_exp * (SUBLANES - 1)) // tm) + n_exp
    e = top_i.reshape(n_tiles, n_a)
    onehot = (e[:, :, None] == jnp.arange(n_exp, dtype=jnp.int32)).astype(jnp.int32)
    csum = jnp.cumsum(onehot, axis=1)
    rank = jnp.sum((csum - 1) * onehot, axis=-1)
    cnt = csum[:, -1, :]
    seg = (cnt + SUBLANES - 1) // SUBLANES * SUBLANES
    loff = jnp.cumsum(seg, axis=1) - seg
    lpos = jnp.sum(onehot * loff[:, None, :], axis=-1) + rank
    group = jnp.sum(seg, axis=0)
    group_pad = (group + tm - 1) // tm * tm
    pad_end = jnp.cumsum(group_pad)
    base = pad_end - group_pad
    gbase = base[None, :] + jnp.cumsum(seg, axis=0) - seg
    zero_col = jnp.zeros((n_tiles, 1), jnp.int32)
    tile_meta = jnp.concatenate([loff, zero_col, gbase, zero_col, seg // SUBLANES, zero_col], axis=1)
    fill_meta = jnp.concatenate([
        jnp.zeros((n_exp + 1,), jnp.int32),
        base + group, pad_end[-1:],
        (group_pad - group) // SUBLANES, n_blocks - pad_end[-1:] // tm])
    meta = jnp.concatenate([tile_meta, fill_meta[None]], axis=0).astype(jnp.int32)
    blk_start = jnp.arange(n_blocks, dtype=jnp.int32) * tm
    blk_e = jnp.minimum(jnp.sum(blk_start[:, None] >= pad_end[None, :], axis=1), n_exp - 1).astype(jnp.int32)
    blk_rows = jnp.clip(base[blk_e] + group[blk_e] - blk_start, 0, tm).astype(jnp.int32)
    return (lpos.astype(jnp.int32).reshape(n_tiles, 1, n_a), meta.reshape(n_tiles + 1, 1, 3 * (n_exp + 1)),
            blk_e, blk_rows, n_blocks)


def _post2_kernel(lpos_ref, tg_ref, meta_ref, meta_nxt_ref, ys_hbm, x_ref, g2_ref, lng_ref, lnb_ref, o_ref,
                  stage_scr, sem, *, tt, n_exp, n_tiles, alpha):
    i = pl.program_id(0)
    slot = i % 2

    @pl.when(i == 0)
    def _():
        stage_scr[...] = jnp.zeros(stage_scr.shape, stage_scr.dtype)

    def chunk_copy(s, src_row, dst_row):
        return pltpu.make_async_copy(ys_hbm.at[pl.ds(src_row, SUBLANES), :],
                                     stage_scr.at[s, pl.ds(dst_row, SUBLANES), :], sem.at[s])

    n_seg = n_exp + 1

    def fetch(m_ref, s):
        for e in range(n_exp):
            loff, gbase, n_chunk = m_ref[0, 0, e], m_ref[0, 0, n_seg + e], m_ref[0, 0, 2 * n_seg + e]

            def start(c, _, loff=loff, gbase=gbase):
                chunk_copy(s, pl.multiple_of(gbase + c * SUBLANES, SUBLANES),
                           pl.multiple_of(loff + c * SUBLANES, SUBLANES)).start()
                return 0
            lax.fori_loop(0, n_chunk, start, 0)

    @pl.when(i == 0)
    def _():
        fetch(meta_ref, 0)

    @pl.when(i + 1 < n_tiles)
    def _():
        fetch(meta_nxt_ref, 1 - slot)

    total = meta_ref[0, 0, 2 * n_seg]
    for e in range(1, n_exp):
        total = total + meta_ref[0, 0, 2 * n_seg + e]

    def wait(c, _):
        chunk_copy(slot, 0, 0).wait()
        return 0
    lax.fori_loop(0, total, wait, 0)

    lpos = lpos_ref[...]
    tg = tg_ref[...]
    col = lax.broadcasted_iota(jnp.int32, (tt, stage_scr.shape[1]), 1)
    sel = jnp.where(col == lpos[:, 0:1], tg[:, 0:1], 0.0)
    for k in range(1, TOP_K):
        sel = sel + jnp.where(col == lpos[:, k:k + 1], tg[:, k:k + 1], 0.0)
    y = jnp.dot(sel.astype(BF16), stage_scr[slot].astype(BF16), preferred_element_type=F32)
    o_ref[...] = _layernorm(alpha * x_ref[...] + g2_ref[0] * y, lng_ref[...], lnb_ref[...])


def _post2(ys, lpos, top_g, meta, x1, mod, ln_g, ln_b, tok_per_mod, alpha, tile_off, tt, n_exp):
    n_tok, d = x1.shape
    n_tiles = n_tok // tt
    last_tile = meta.shape[0] - 2
    stage_rows = -(-(tt * TOP_K + n_exp * SUBLANES) // LANES) * LANES
    cur = lambda i: (tile_off + i, 0, 0)
    nxt = lambda i: (jnp.minimum(tile_off + i + 1, last_tile), 0, 0)
    smem = lambda a, fn: pl.BlockSpec((1, 1, a.shape[-1]), fn, memory_space=pltpu.SMEM)
    row = lambda a: pl.BlockSpec((1, a.shape[-1]), lambda i: (0, 0))
    return pl.pallas_call(
        functools.partial(_post2_kernel, tt=tt, n_exp=n_exp, n_tiles=n_tiles, alpha=alpha),
        out_shape=jax.ShapeDtypeStruct((n_tok, d), F32),
        grid=(n_tiles,),
        in_specs=[
            pl.BlockSpec((tt, TOP_K), lambda i: (tile_off + i, 0)),
            pl.BlockSpec((tt, LANES), lambda i: (i, 0)),
            smem(meta, cur), smem(meta, nxt),
            pl.BlockSpec(memory_space=pl.ANY),
            pl.BlockSpec((tt, d), lambda i: (i, 0)),
            pl.BlockSpec((1, 1, d), lambda i: ((i * tt) // tok_per_mod, 0, 5)),
            row(ln_g), row(ln_b),
        ],
        out_specs=pl.BlockSpec((tt, d), lambda i: (i, 0)),
        scratch_shapes=[
            pltpu.VMEM((2, stage_rows, d), F32),
            pltpu.SemaphoreType.DMA((2,)),
        ],
        compiler_params=_cparams(("arbitrary",)),
        name="post2",
    )(lpos, top_g, meta, meta, ys, x1, mod, ln_g, ln_b)


def _router_weights(w_router, b_router):
    d, n_exp = w_router.shape
    hi = w_router.astype(BF16)
    lo = (w_router - hi.astype(F32)).astype(BF16)
    pad = lambda a: jnp.pad(a, ((0, 0), (0, LANES - n_exp)))
    return jnp.concatenate([pad(hi), pad(lo)], axis=1), jnp.pad(b_router, (0, LANES - n_exp)).reshape(1, LANES)


def kernel(x_prompt, x_sample, cache_k, cache_v, state_h_fwd, state_h_bwd, c, c_ctx, w_ada, b_ada, w_in, b_in,
           conv_w, conv_b, rg_wa, rg_ba, rg_wx, rg_bx, rg_lam, rpb, w_rnn_out, w_na_out, w_o, b_o,
           ln1_g, ln1_b, ln2_g, ln2_b, w_router, b_router, w_gu, b_gu, w_dn, b_dn):
    depth, d, _ = w_in.shape
    bsz, seq, _ = x_prompt.shape
    dbsz, dseq, _ = x_sample.shape
    n_heads, hd = cache_k.shape[2], cache_k.shape[4]
    d_na = n_heads * hd
    d_rnn = conv_w.shape[2]
    n_exp = w_router.shape[2]
    alpha = (2 * depth) ** 0.25
    n_p, n_s = bsz * seq, dbsz * dseq

    xp = x_prompt.reshape(n_p, d)
    xs = x_sample.reshape(n_s, d)
    n_cond = -(-(1 + dbsz) // SUBLANES) * SUBLANES
    cond = jnp.concatenate([c_ctx[None], c, jnp.zeros((n_cond - 1 - dbsz, d), F32)], axis=0)
    ks, vs, hfs, hbs = [], [], [], []
    for l in range(depth):
        mod = _ada(cond, w_ada[l], b_ada[l]).reshape(n_cond, 1, 6 * d)
        mod_p, mod_s = mod[0:1], mod[1:1 + dbsz]
        w_in_l = w_in[l].astype(BF16)
        w_gates = jnp.concatenate([rg_wa[l], rg_wx[l]], axis=-1).astype(BF16)
        w_rnn_l, w_na_l, w_o_l = w_rnn_out[l].astype(BF16), w_na_out[l].astype(BF16), w_o[l].astype(BF16)
        w_rt, b_rt = _router_weights(w_router[l], b_router[l])
        row = lambda a: a[l].reshape(1, -1)

        def mixer(x, mod_g, n_b, t, h0, tok_per_mod, is_ctx):
            xr, qkv, g = _inproj(x, mod_g, w_in_l, b_in[l], d_rnn, d_na, tok_per_mod, F32 if is_ctx else BF16)
            yf, yb, h_last = _rglru(xr.reshape(n_b, t, d_rnn), conv_w[l], conv_b[l], w_gates,
                                    rg_ba[l], rg_bx[l], rg_lam[l], h0, is_ctx)
            qkv3 = qkv.reshape(n_b, t, 3 * d_na)
            if is_ctx:
                ya = _cattn(qkv3, n_heads, hd)
            else:
                ya = _nattn(qkv3, cache_k[:, l], cache_v[:, l], rpb[l], n_heads, hd)
            merged = _merge(yf.reshape(n_b * t, d_rnn), yb.reshape(n_b * t, d_rnn), ya.reshape(n_b * t, d_na),
                            g, w_rnn_l, w_na_l)
            x1, u2, ti, tg = _post1(merged, x, mod_g, w_o_l, row(b_o), row(ln1_g), row(ln1_b), w_rt, b_rt,
                                    tok_per_mod, alpha, n_exp)
            return x1, u2, ti, tg, qkv3, h_last

        x1p, u2p, tip, tgp, qkv_p, hl_p = mixer(xp, mod_p, bsz, seq, jnp.zeros((bsz, 2, d_rnn), F32), n_p, True)
        h0_s = jnp.stack([state_h_fwd[:, l], state_h_bwd[:, l]], axis=1)
        x1s, u2s, tis, tgs, _, _ = mixer(xs, mod_s, dbsz, dseq, h0_s, dseq, False)

        n_all = n_p + n_s
        tt = _pick(np.gcd(n_p, dseq), MOE_TT)
        tm = MOE_TM
        u2 = jnp.concatenate([u2p, u2s], axis=0)
        top_i = jnp.concatenate([tip[:, :TOP_K], tis[:, :TOP_K]], axis=0)
        lpos, meta, blk_e, blk_rows, n_blocks = _route(top_i, n_exp, tt, tm)
        xsort = _dispatch(u2, lpos, meta, n_blocks * tm, tt, tm, n_exp)
        ys = _moe(xsort, blk_e, blk_rows, w_gu[l], b_gu[l], w_dn[l], b_dn[l], tm)
        lpos_tok = lpos.reshape(n_all, TOP_K)
        xp = _post2(ys, lpos_tok, tgp, meta, x1p, mod_p, row(ln2_g), row(ln2_b), n_p, alpha, 0, tt, n_exp)
        xs = _post2(ys, lpos_tok, tgs, meta, x1s, mod_s, row(ln2_g), row(ln2_b), dseq, alpha, n_p // tt, tt, n_exp)

        heads = lambda a: a.reshape(bsz, seq, n_heads, hd).transpose(0, 2, 1, 3)
        ks.append(heads(qkv_p[:, :, d_na:2 * d_na]))
        vs.append(heads(qkv_p[:, :, 2 * d_na:]))
        hfs.append(hl_p[:, 0])
        hbs.append(hl_p[:, 1])

    return (xp.reshape(bsz, seq, d), xs.reshape(dbsz, dseq, d),
            jnp.stack(ks, axis=1), jnp.stack(vs, axis=1), jnp.stack(hfs, axis=1), jnp.stack(hbs, axis=1))
```

```python
import functools

import numpy as np
import jax
import jax.numpy as jnp
from jax import lax
from jax.experimental import pallas as pl
from jax.experimental.pallas import tpu as pltpu

F32 = jnp.float32
BF16 = jnp.bfloat16

GRID_W = 64
WIN_ROWS = 8
WIN_COLS = 16
CONV_W = 4
RG_C = 8.0
TOP_K = 4
SWIGLU_LIMIT = 7.0
SWIGLU_ALPHA = 1.702
LN_EPS = 1e-5
NEG_INF = -1e30

VMEM_LIMIT_BYTES = 56 * 1024 * 1024
SUBLANES = 8
LANES = 128

NATTN_ROWS = 4
NATTN_SUB = 4
MOE_TT = 256
MOE_TM = 1024
MOE_TF = 256
MOE_ROW_CHUNKS = 2
ROW_CHUNKS = 2


def _cparams(sem):
    return pltpu.CompilerParams(dimension_semantics=sem, vmem_limit_bytes=VMEM_LIMIT_BYTES)


def _pick(n, pref):
    t = min(n, pref)
    while n % t:
        t //= 2
    return t


def _ada_kernel(c_ref, w_ref, b_ref, o_ref):
    c = c_ref[...]
    s = c * jax.nn.sigmoid(c)
    o_ref[...] = jnp.dot(s.astype(BF16), w_ref[...].astype(BF16), preferred_element_type=F32) + b_ref[...]


def _ada(cond, w_ada, b_ada):
    nb, d = cond.shape
    n = w_ada.shape[1]
    tn = _pick(n, 1024)
    return pl.pallas_call(
        _ada_kernel,
        out_shape=jax.ShapeDtypeStruct((nb, n), F32),
        grid=(n // tn,),
        in_specs=[
            pl.BlockSpec((nb, d), lambda j: (0, 0)),
            pl.BlockSpec((d, tn), lambda j: (0, j)),
            pl.BlockSpec((1, tn), lambda j: (0, j)),
        ],
        out_specs=pl.BlockSpec((nb, tn), lambda j: (0, j)),
        compiler_params=_cparams(("arbitrary",)),
        name="ada",
    )(cond, w_ada, b_ada.reshape(1, n))


def _inproj_kernel(x_ref, sh_ref, sc_ref, w_ref, b_ref, xr_ref, qkv_ref, g_ref, u_scr, *, n_xr, n_qkv):
    n = pl.program_id(1)

    @pl.when(n == 0)
    def _():
        u = x_ref[...] * (1.0 + sc_ref[0]) + sh_ref[0]
        u_scr[...] = u.astype(BF16)

    def project(o_ref):
        rows = u_scr.shape[0] // ROW_CHUNKS
        for c in range(ROW_CHUNKS):
            sl = pl.ds(c * rows, rows)
            acc = jnp.dot(u_scr[sl, :], w_ref[...], preferred_element_type=F32) + b_ref[...]
            o_ref[sl, :] = acc.astype(o_ref.dtype)

    @pl.when(n < n_xr)
    def _():
        project(xr_ref)

    @pl.when((n >= n_xr) & (n < n_xr + n_qkv))
    def _():
        project(qkv_ref)

    @pl.when(n >= n_xr + n_qkv)
    def _():
        project(g_ref)


def _inproj(x, mod, w_in, b_in, d_rnn, d_na, tok_per_mod, qkv_dtype):
    n_tok, d = x.shape
    d_in = w_in.shape[1]
    tn = _pick(np.gcd(np.gcd(d_rnn, d_na), d), 1024)
    tm = _pick(tok_per_mod, 512)
    n_xr, n_qkv, n_g = d_rnn // tn, 3 * d_na // tn, 2 * d // tn
    assert (n_xr + n_qkv + n_g) * tn == d_in

    def mod_map(col):
        return lambda i, j: ((i * tm) // tok_per_mod, 0, col)

    return pl.pallas_call(
        functools.partial(_inproj_kernel, n_xr=n_xr, n_qkv=n_qkv),
        out_shape=(
            jax.ShapeDtypeStruct((n_tok, d_rnn), F32),
            jax.ShapeDtypeStruct((n_tok, 3 * d_na), qkv_dtype),
            jax.ShapeDtypeStruct((n_tok, 2 * d), BF16),
        ),
        grid=(n_tok // tm, d_in // tn),
        in_specs=[
            pl.BlockSpec((tm, d), lambda i, j: (i, 0)),
            pl.BlockSpec((1, 1, d), mod_map(0)),
            pl.BlockSpec((1, 1, d), mod_map(1)),
            pl.BlockSpec((d, tn), lambda i, j: (0, j)),
            pl.BlockSpec((1, tn), lambda i, j: (0, j)),
        ],
        out_specs=(
            pl.BlockSpec((tm, tn), lambda i, j: (i, jnp.minimum(j, n_xr - 1))),
            pl.BlockSpec((tm, tn), lambda i, j: (i, jnp.clip(j - n_xr, 0, n_qkv - 1))),
            pl.BlockSpec((tm, tn), lambda i, j: (i, jnp.clip(j - n_xr - n_qkv, 0, n_g - 1))),
        ),
        scratch_shapes=[pltpu.VMEM((tm, d), BF16)],
        compiler_params=_cparams(("arbitrary", "arbitrary")),
        name="inproj",
    )(x, mod, mod, w_in, b_in.reshape(1, d_in))


def _rglru_kernel(xf_ref, xfp_ref, xfn_ref, xb_ref, xbp_ref, xbn_ref,
                  cw_ref, cb_ref, w_ref, ba_ref, bx_ref, lam_ref, h0_ref,
                  yf_ref, yb_ref, hl_ref,
                  xext_scr, a_scr, b_scr, carry_scr, *, t_tile, n_t, seq_len, reset):
    j = pl.program_id(2)
    c = xf_ref.shape[-1]
    n_grp = t_tile // SUBLANES

    @pl.when(j == 0)
    def _():
        carry_scr[...] = h0_ref[0]

    row = lax.broadcasted_iota(jnp.int32, (t_tile, 1), 0)
    sub3 = lax.broadcasted_iota(jnp.int32, (1, SUBLANES, 1), 1)

    def direction(d, x_ref, xp_ref, xn_ref, y_ref):
        jt = j if d == 0 else n_t - 1 - j
        xext_scr[0:SUBLANES, :] = jnp.where(jt > 0, xp_ref[0], 0.0)
        xext_scr[SUBLANES:SUBLANES + t_tile, :] = x_ref[0]
        xext_scr[SUBLANES + t_tile:2 * SUBLANES + t_tile, :] = jnp.where(jt < n_t - 1, xn_ref[0], 0.0)
        left = CONV_W // 2
        xc = cb_ref[...] + xext_scr[pl.ds(SUBLANES - left, t_tile), :] * cw_ref[0:1, :]
        for k in range(1, CONV_W):
            xc = xc + xext_scr[pl.ds(SUBLANES - left + k, t_tile), :] * cw_ref[k:k + 1, :]
        pre = jnp.dot(xc.astype(BF16), w_ref[d, 0], preferred_element_type=F32)
        r_gate = jax.nn.sigmoid(pre[:, :c] + ba_ref[d:d + 1, :])
        i_gate = jax.nn.sigmoid(pre[:, c:] + bx_ref[d:d + 1, :])
        lam = lam_ref[d:d + 1, :]
        softplus_neg_lam = jnp.maximum(-lam, 0.0) + jnp.log1p(jnp.exp(-jnp.abs(lam)))
        log_a = -RG_C * r_gate * softplus_neg_lam
        a = jnp.exp(log_a)
        om = 1.0 - a * a
        mult = jnp.where(om > 0.0, om * lax.rsqrt(om), 0.0)
        if reset:
            first = 0 if d == 0 else seq_len - 1
            mult = jnp.where(jt * t_tile + row == first, 1.0, mult)
        b = mult * i_gate * xc
        a = a.reshape(n_grp, SUBLANES, c)
        b = b.reshape(n_grp, SUBLANES, c)
        for s in (1, 2, 4):
            if d == 0:
                a_sh = pltpu.roll(a, s, axis=1)
                b_sh = pltpu.roll(b, s, axis=1)
                m = sub3 >= s
            else:
                a_sh = pltpu.roll(a, SUBLANES - s, axis=1)
                b_sh = pltpu.roll(b, SUBLANES - s, axis=1)
                m = sub3 < SUBLANES - s
            b = jnp.where(m, b + a * b_sh, b)
            a = jnp.where(m, a * a_sh, a)
        a_scr[...] = a.reshape(t_tile, c)
        b_scr[...] = b.reshape(t_tile, c)

        def body(g, h):
            gg = g if d == 0 else n_grp - 1 - g
            off = pl.multiple_of(gg * SUBLANES, SUBLANES)
            h_rows = b_scr[pl.ds(off, SUBLANES), :] + a_scr[pl.ds(off, SUBLANES), :] * h
            y_ref[0, pl.ds(off, SUBLANES), :] = h_rows.astype(y_ref.dtype)
            return h_rows[SUBLANES - 1:SUBLANES, :] if d == 0 else h_rows[0:1, :]

        h = lax.fori_loop(0, n_grp, body, carry_scr[d:d + 1, :], unroll=4)
        carry_scr[d:d + 1, :] = h
        hl_ref[0, d:d + 1, :] = h

    direction(0, xf_ref, xfp_ref, xfn_ref, yf_ref)
    direction(1, xb_ref, xbp_ref, xbn_ref, yb_ref)


def _rglru(xr, conv_w, conv_b, w_gates, rg_ba, rg_bx, rg_lam, h0, reset):
    bsz, t, d_rnn = xr.shape
    n_blk, c = w_gates.shape[1], w_gates.shape[2]
    t_tile = _pick(t, 1024)
    n_t = t // t_tile
    tb = t_tile // SUBLANES
    last_blk = t // SUBLANES - 1

    def main(rev):
        return lambda b, n, j: (b, (n_t - 1 - j) if rev else j, n)

    def prev(rev):
        return lambda b, n, j: (b, jnp.maximum(((n_t - 1 - j) if rev else j) * tb - 1, 0), n)

    def nxt(rev):
        return lambda b, n, j: (b, jnp.minimum((((n_t - 1 - j) if rev else j) + 1) * tb, last_blk), n)

    vec = lambda rows: pl.BlockSpec((rows, c), lambda b, n, j: (0, n))
    return pl.pallas_call(
        functools.partial(_rglru_kernel, t_tile=t_tile, n_t=n_t, seq_len=t, reset=reset),
        out_shape=(
            jax.ShapeDtypeStruct((bsz, t, d_rnn), BF16),
            jax.ShapeDtypeStruct((bsz, t, d_rnn), BF16),
            jax.ShapeDtypeStruct((bsz, 2, d_rnn), F32),
        ),
        grid=(bsz, n_blk, n_t),
        in_specs=[
            pl.BlockSpec((1, t_tile, c), main(False)),
            pl.BlockSpec((1, SUBLANES, c), prev(False)),
            pl.BlockSpec((1, SUBLANES, c), nxt(False)),
            pl.BlockSpec((1, t_tile, c), main(True)),
            pl.BlockSpec((1, SUBLANES, c), prev(True)),
            pl.BlockSpec((1, SUBLANES, c), nxt(True)),
            vec(CONV_W),
            vec(1),
            pl.BlockSpec((2, 1, c, 2 * c), lambda b, n, j: (0, n, 0, 0)),
            vec(2), vec(2), vec(2),
            pl.BlockSpec((1, 2, c), lambda b, n, j: (b, 0, n)),
        ],
        out_specs=(
            pl.BlockSpec((1, t_tile, c), main(False)),
            pl.BlockSpec((1, t_tile, c), main(True)),
            pl.BlockSpec((1, 2, c), lambda b, n, j: (b, 0, n)),
        ),
        scratch_shapes=[
            pltpu.VMEM((t_tile + 2 * SUBLANES, c), F32),
            pltpu.VMEM((t_tile, c), F32),
            pltpu.VMEM((t_tile, c), F32),
            pltpu.VMEM((2, c), F32),
        ],
        compiler_params=_cparams(("arbitrary", "arbitrary", "arbitrary")),
        name="rglru",
    )(xr, xr, xr, xr, xr, xr, conv_w, conv_b.reshape(1, d_rnn), w_gates, rg_ba, rg_bx, rg_lam, h0)


def _qk(q, k):
    return lax.dot_general(q, k, (((1,), (1,)), ((), ())), preferred_element_type=F32)


def _cattn_kernel(q_ref, k_ref, v_ref, o_ref, *, n_heads, hd, scale):
    for h in range(n_heads):
        sl = slice(h * hd, (h + 1) * hd)
        q = q_ref[0, :, sl].astype(BF16)
        k = k_ref[0, :, sl].astype(BF16)
        v = v_ref[0, :, sl].astype(BF16)
        s = _qk(q, k) * scale
        p = jnp.exp(s - jnp.max(s, axis=-1, keepdims=True))
        l = jnp.sum(p, axis=-1, keepdims=True)
        o = jnp.dot(p.astype(BF16), v, preferred_element_type=F32) / l
        o_ref[0, :, sl] = o.astype(o_ref.dtype)


def _cattn(qkv, n_heads, hd):
    bsz, s, _ = qkv.shape
    d_na = n_heads * hd
    spec = lambda col: pl.BlockSpec((1, s, d_na), lambda b: (b, 0, col))
    return pl.pallas_call(
        functools.partial(_cattn_kernel, n_heads=n_heads, hd=hd, scale=hd ** -0.5),
        out_shape=jax.ShapeDtypeStruct((bsz, s, d_na), BF16),
        grid=(bsz,),
        in_specs=[spec(0), spec(1), spec(2)],
        out_specs=pl.BlockSpec((1, s, d_na), lambda b: (b, 0, 0)),
        compiler_params=_cparams(("arbitrary",)),
        name="cattn",
    )(qkv, qkv, qkv)


def _nattn_window_start(rb, rows, r_blk, w_blk):
    return jnp.clip(rb * r_blk - WIN_ROWS // 2, 0, rows - w_blk)


def _nattn_kernel(q_ref, k_ref, v_ref, ck_ref, cv_ref, bias_ref, o_ref, *, hd, scale, rows, r_blk, w_blk, n_sub):
    step = pl.program_id(1)
    n_rb = rows // r_blk
    tq = r_blk * GRID_W
    for s in range(n_sub):
        rb = step * n_sub + s
        pat = jnp.where(rb == 0, 0, jnp.where(rb == n_rb - 1, 2, 1))
        w0 = _nattn_window_start(rb, rows, r_blk, w_blk)
        start = pl.multiple_of(w0 * GRID_W, GRID_W)
        k_win = k_ref[0, pl.ds(start, w_blk * GRID_W), :]
        v_win = v_ref[0, pl.ds(start, w_blk * GRID_W), :]
        q = q_ref[0, s * tq:(s + 1) * tq, :]
        outs = []
        for h in range(q.shape[-1] // hd):
            sl = slice(h * hd, (h + 1) * hd)
            qh = q[:, sl]
            s_lat = _qk(qh, k_win[:, sl]) * scale + bias_ref[h, pat]
            s_ctx = _qk(qh, ck_ref[0, h].astype(BF16)) * scale
            m = jnp.maximum(jnp.max(s_lat, axis=-1, keepdims=True), jnp.max(s_ctx, axis=-1, keepdims=True))
            p_lat = jnp.exp(s_lat - m)
            p_ctx = jnp.exp(s_ctx - m)
            l = jnp.sum(p_lat, axis=-1, keepdims=True) + jnp.sum(p_ctx, axis=-1, keepdims=True)
            o = (jnp.dot(p_lat.astype(BF16), v_win[:, sl], preferred_element_type=F32)
                 + jnp.dot(p_ctx.astype(BF16), cv_ref[0, h].astype(BF16), preferred_element_type=F32))
            outs.append(o / l)
        o_ref[0, s * tq:(s + 1) * tq, :] = jnp.concatenate(outs, axis=-1).astype(o_ref.dtype)


def _nattn_bias(rpb, rows, r_blk, w_blk):
    n_rb = rows // r_blk
    wr = min(WIN_ROWS, rows)
    n_heads = rpb.shape[0]
    col = np.arange(GRID_W)
    cstart = np.clip(col - WIN_COLS // 2, 0, GRID_W - WIN_COLS)
    col_ok = (col[None, :] >= cstart[:, None]) & (col[None, :] < cstart[:, None] + WIN_COLS)
    rp = jnp.pad(rpb.astype(F32), ((0, 0), (0, 0), (GRID_W, GRID_W)))
    shifted = [rp[:, :, GRID_W + WIN_COLS - 1 - qc:2 * GRID_W + WIN_COLS - 1 - qc] for qc in range(GRID_W)]
    tiles = jnp.where(col_ok[None, None], jnp.stack(shifted, axis=2), NEG_INF)
    masked = jnp.full((n_heads, GRID_W, GRID_W), NEG_INF, F32)
    pats = []
    for rb in (0, min(1, n_rb - 1), n_rb - 1):
        r0 = rb * r_blk
        w0 = int(np.clip(r0 - WIN_ROWS // 2, 0, rows - w_blk))
        strips = []
        for i in range(r_blk):
            qrow = r0 + i
            rstart = int(np.clip(qrow - wr // 2, 0, rows - wr))
            strip = []
            for jj in range(w_blk):
                krow = w0 + jj
                ok = rstart <= krow < rstart + wr
                strip.append(tiles[:, krow - qrow + WIN_ROWS - 1] if ok else masked)
            strips.append(jnp.concatenate(strip, axis=-1))
        pats.append(jnp.concatenate(strips, axis=-2))
    return jnp.stack(pats, axis=1)


def _nattn(qkv, ck, cv, rpb, n_heads, hd):
    bsz, t, _ = qkv.shape
    d_na = n_heads * hd
    rows = t // GRID_W
    r_blk = min(NATTN_ROWS, rows)
    w_blk = min(r_blk + WIN_ROWS - 1, rows)
    n_rb = rows // r_blk
    for rb in range(1, n_rb - 1):
        assert 0 <= rb * r_blk - WIN_ROWS // 2 <= rows - w_blk
    hp = LANES // hd
    n_pair = d_na // LANES
    past = ck.shape[2]
    bias = _nattn_bias(rpb, rows, r_blk, w_blk)
    n_sub = _pick(n_rb, NATTN_SUB)
    tq, tk = n_sub * r_blk * GRID_W, w_blk * GRID_W

    return pl.pallas_call(
        functools.partial(_nattn_kernel, hd=hd, scale=hd ** -0.5, rows=rows, r_blk=r_blk, w_blk=w_blk,
                          n_sub=n_sub),
        out_shape=jax.ShapeDtypeStruct((bsz, t, d_na), BF16),
        grid=(n_pair, n_rb // n_sub, bsz),
        in_specs=[
            pl.BlockSpec((1, tq, LANES), lambda p, rb, b: (b, rb, p)),
            pl.BlockSpec((1, t, LANES), lambda p, rb, b: (b, 0, n_pair + p)),
            pl.BlockSpec((1, t, LANES), lambda p, rb, b: (b, 0, 2 * n_pair + p)),
            pl.BlockSpec((1, hp, past, hd), lambda p, rb, b: (b, p, 0, 0)),
            pl.BlockSpec((1, hp, past, hd), lambda p, rb, b: (b, p, 0, 0)),
            pl.BlockSpec((hp, 3, r_blk * GRID_W, tk), lambda p, rb, b: (p, 0, 0, 0)),
        ],
        out_specs=pl.BlockSpec((1, tq, LANES), lambda p, rb, b: (b, rb, p)),
        compiler_params=_cparams(("arbitrary", "arbitrary", "arbitrary")),
        name="nattn",
    )(qkv, qkv, qkv, ck, cv, bias)


def _merge_kernel(yf_ref, yb_ref, ya_ref, ga_ref, gb_ref, wr_ref, wa_ref, o_ref):
    rows = o_ref.shape[0] // ROW_CHUNKS
    for c in range(ROW_CHUNKS):
        sl = pl.ds(c * rows, rows)
        y_rnn = (yf_ref[sl, :].astype(F32) + yb_ref[sl, :].astype(F32)).astype(BF16)
        t_rnn = jnp.dot(y_rnn, wr_ref[...], preferred_element_type=F32)
        t_na = jnp.dot(ya_ref[sl, :], wa_ref[...], preferred_element_type=F32)
        merged = (jax.nn.sigmoid(ga_ref[sl, :].astype(F32)) * t_rnn
                  + jax.nn.sigmoid(gb_ref[sl, :].astype(F32)) * t_na)
        o_ref[sl, :] = merged.astype(o_ref.dtype)


def _merge(yf, yb, ya, g, w_rnn_out, w_na_out):
    n_tok, d_rnn = yf.shape
    d_na = ya.shape[1]
    d = w_rnn_out.shape[1]
    tm = _pick(n_tok, 512)
    tn = _pick(d, 1024)
    n_n = d // tn
    return pl.pallas_call(
        _merge_kernel,
        out_shape=jax.ShapeDtypeStruct((n_tok, d), BF16),
        grid=(n_tok // tm, n_n),
        in_specs=[
            pl.BlockSpec((tm, d_rnn), lambda i, j: (i, 0)),
            pl.BlockSpec((tm, d_rnn), lambda i, j: (i, 0)),
            pl.BlockSpec((tm, d_na), lambda i, j: (i, 0)),
            pl.BlockSpec((tm, tn), lambda i, j: (i, j)),
            pl.BlockSpec((tm, tn), lambda i, j: (i, n_n + j)),
            pl.BlockSpec((d_rnn, tn), lambda i, j: (0, j)),
            pl.BlockSpec((d_na, tn), lambda i, j: (0, j)),
        ],
        out_specs=pl.BlockSpec((tm, tn), lambda i, j: (i, j)),
        compiler_params=_cparams(("arbitrary", "arbitrary")),
        name="merge",
    )(yf, yb, ya, g, g, w_rnn_out, w_na_out)


def _layernorm(z, g, b):
    mu = jnp.mean(z, axis=-1, keepdims=True)
    zc = z - mu
    var = jnp.mean(zc * zc, axis=-1, keepdims=True)
    return zc * lax.rsqrt(var + LN_EPS) * g + b


def _pack_bf16_pairs(x):
    half = x.shape[-1] // 2
    lo = lax.bitcast_convert_type(x[:, :half].astype(BF16).astype(F32), jnp.uint32)
    hi = lax.bitcast_convert_type(x[:, half:].astype(BF16).astype(F32), jnp.uint32)
    return (lo >> 16) | (hi & jnp.uint32(0xFFFF0000))


def _unpack_bf16_pairs(w):
    lo = lax.bitcast_convert_type(w << 16, F32).astype(BF16)
    hi = lax.bitcast_convert_type(w & jnp.uint32(0xFFFF0000), F32).astype(BF16)
    return lo, hi


def _post1_kernel(m_ref, x_ref, g1_ref, sh2_ref, sc2_ref, wo_ref, bo_ref, lng_ref, lnb_ref, wrt_ref, brt_ref,
                  x1_ref, u2_ref, ti_ref, tg_ref, *, alpha, n_experts):
    rows = x_ref.shape[0] // ROW_CHUNKS
    for c in range(ROW_CHUNKS):
        _post1_rows(pl.ds(c * rows, rows), m_ref, x_ref, g1_ref, sh2_ref, sc2_ref, wo_ref, bo_ref, lng_ref, lnb_ref,
                    wrt_ref, brt_ref, x1_ref, u2_ref, ti_ref, tg_ref, alpha, n_experts)


def _post1_rows(sl, m_ref, x_ref, g1_ref, sh2_ref, sc2_ref, wo_ref, bo_ref, lng_ref, lnb_ref, wrt_ref, brt_ref,
                x1_ref, u2_ref, ti_ref, tg_ref, alpha, n_experts):
    mix = jnp.dot(m_ref[sl, :], wo_ref[...], preferred_element_type=F32) + bo_ref[...]
    x1 = _layernorm(alpha * x_ref[sl, :] + g1_ref[0] * mix, lng_ref[...], lnb_ref[...])
    x1_ref[sl, :] = x1
    u2 = x1 * (1.0 + sc2_ref[0]) + sh2_ref[0]
    u2_ref[sl, :] = _pack_bf16_pairs(u2)
    hi = u2.astype(BF16)
    lo = (u2 - hi.astype(F32)).astype(BF16)
    ph = jnp.dot(hi, wrt_ref[...], preferred_element_type=F32)
    plo = jnp.dot(lo, wrt_ref[...], preferred_element_type=F32)
    logits = (ph[:, :LANES] + ph[:, LANES:]) + (plo[:, :LANES] + plo[:, LANES:]) + brt_ref[...]
    lane = lax.broadcasted_iota(jnp.int32, logits.shape, 1)
    lane_f = lane.astype(F32)
    logits = jnp.where(lane < n_experts, logits, NEG_INF)
    top_i = jnp.zeros(logits.shape, jnp.int32)
    top_e = jnp.zeros(logits.shape, F32)
    v0 = None
    denom = None
    for k in range(TOP_K):
        v = jnp.max(logits, axis=-1, keepdims=True)
        idx = jnp.min(jnp.where(logits == v, lane_f, float(LANES)), axis=-1, keepdims=True).astype(jnp.int32)
        if k == 0:
            v0 = v
        e = jnp.exp(v - v0)
        denom = e if k == 0 else denom + e
        top_i = jnp.where(lane == k, idx, top_i)
        top_e = jnp.where(lane == k, e, top_e)
        logits = jnp.where(lane == idx, NEG_INF, logits)
    ti_ref[sl, :] = top_i
    tg_ref[sl, :] = top_e / denom


def _post1(merged, x, mod, w_o, b_o, ln_g, ln_b, w_rt, b_rt, tok_per_mod, alpha, n_experts):
    n_tok, d = x.shape
    tm = _pick(tok_per_mod, 512)

    def mod_map(col):
        return lambda i: ((i * tm) // tok_per_mod, 0, col)

    row = lambda a: pl.BlockSpec((1, a.shape[-1]), lambda i: (0, 0))
    tile = pl.BlockSpec((tm, d), lambda i: (i, 0))
    small = pl.BlockSpec((tm, LANES), lambda i: (i, 0))
    return pl.pallas_call(
        functools.partial(_post1_kernel, alpha=alpha, n_experts=n_experts),
        out_shape=(
            jax.ShapeDtypeStruct((n_tok, d), F32),
            jax.ShapeDtypeStruct((n_tok, d // 2), jnp.uint32),
            jax.ShapeDtypeStruct((n_tok, LANES), jnp.int32),
            jax.ShapeDtypeStruct((n_tok, LANES), F32),
        ),
        grid=(n_tok // tm,),
        in_specs=[
            tile, tile,
            pl.BlockSpec((1, 1, d), mod_map(2)),
            pl.BlockSpec((1, 1, d), mod_map(3)),
            pl.BlockSpec((1, 1, d), mod_map(4)),
            pl.BlockSpec((d, d), lambda i: (0, 0)),
            row(b_o), row(ln_g), row(ln_b),
            pl.BlockSpec((d, 2 * LANES), lambda i: (0, 0)),
            row(b_rt),
        ],
        out_specs=(tile, pl.BlockSpec((tm, d // 2), lambda i: (i, 0)), small, small),
        compiler_params=_cparams(("arbitrary",)),
        name="post1",
    )(merged, x, mod, mod, mod, w_o, b_o, ln_g, ln_b, w_rt, b_rt)


def _dispatch_kernel(lpos_ref, meta_ref, xa_ref, xb_ref, xs_hbm, stage_scr, sem, bsem, nprev_scr,
                     *, tt, tm, n_exp, n_tiles_a, n_tiles):
    i = pl.program_id(0)
    slot = i % 2
    n_seg = n_exp + 1
    is_fill = i == n_tiles

    @pl.when(i == 0)
    def _():
        stage_scr[...] = jnp.zeros(stage_scr.shape, stage_scr.dtype)

    @pl.when(is_fill)
    def _():
        stage_scr[slot] = jnp.zeros(stage_scr.shape[1:], stage_scr.dtype)

    def place_from(x_ref):
        def place(t, _):
            row = x_ref[pl.ds(t, 1), :]
            for k in range(TOP_K):
                stage_scr[slot, pl.ds(lpos_ref[0, 0, t * TOP_K + k], 1), :] = row
            return 0
        lax.fori_loop(0, tt, place, 0, unroll=2)

    @pl.when(i < n_tiles_a)
    def _():
        place_from(xa_ref)

    @pl.when((i >= n_tiles_a) & jnp.logical_not(is_fill))
    def _():
        place_from(xb_ref)

    def chunk_copy(s, src_row, dst_row):
        return pltpu.make_async_copy(stage_scr.at[s, pl.ds(src_row, SUBLANES), :],
                                     xs_hbm.at[pl.ds(dst_row, SUBLANES), :], sem.at[s])

    def block_copy(dst_row):
        return pltpu.make_async_copy(stage_scr.at[slot, pl.ds(0, tm), :], xs_hbm.at[pl.ds(dst_row, tm), :], bsem)

    src_step = jnp.where(is_fill, 0, SUBLANES)
    total = 0
    for e in range(n_exp):
        loff, gbase, n_chunk = meta_ref[0, 0, e], meta_ref[0, 0, n_seg + e], meta_ref[0, 0, 2 * n_seg + e]

        def start(c, _, loff=loff, gbase=gbase):
            chunk_copy(slot, pl.multiple_of(loff + c * src_step, SUBLANES),
                       pl.multiple_of(gbase + c * SUBLANES, SUBLANES)).start()
            return 0
        lax.fori_loop(0, n_chunk, start, 0)
        total = total + n_chunk

    def drain(s, n):
        def wait(c, _):
            chunk_copy(s, 0, 0).wait()
            return 0
        lax.fori_loop(0, n, wait, 0)

    @pl.when(i > 0)
    def _():
        drain(1 - slot, nprev_scr[0])
    nprev_scr[0] = total

    @pl.when(is_fill)
    def _():
        first_row, n_blk = meta_ref[0, 0, n_seg + n_exp], meta_ref[0, 0, 2 * n_seg + n_exp]

        def start(b, _):
            block_copy(pl.multiple_of(first_row + b * tm, tm)).start()
            return 0
        lax.fori_loop(0, n_blk, start, 0)

        def wait(b, _):
            block_copy(0).wait()
            return 0
        lax.fori_loop(0, n_blk, wait, 0)
        drain(slot, total)


def _dispatch(u_a, u_b, lpos, meta, n_slots, tt, tm, n_exp):
    dw = u_a.shape[1]
    n_tiles_a, n_tiles_b = u_a.shape[0] // tt, u_b.shape[0] // tt
    n_tiles = n_tiles_a + n_tiles_b
    stage_rows = max(tt * TOP_K + n_exp * SUBLANES, tm)
    return pl.pallas_call(
        functools.partial(_dispatch_kernel, tt=tt, tm=tm, n_exp=n_exp, n_tiles_a=n_tiles_a, n_tiles=n_tiles),
        out_shape=jax.ShapeDtypeStruct((n_slots, dw), u_a.dtype),
        grid=(n_tiles + 1,),
        in_specs=[
            pl.BlockSpec((1, 1, lpos.shape[-1]), lambda i: (jnp.minimum(i, n_tiles - 1), 0, 0),
                         memory_space=pltpu.SMEM),
            pl.BlockSpec((1, 1, meta.shape[-1]), lambda i: (i, 0, 0), memory_space=pltpu.SMEM),
            pl.BlockSpec((tt, dw), lambda i: (jnp.minimum(i, n_tiles_a - 1), 0)),
            pl.BlockSpec((tt, dw), lambda i: (jnp.clip(i - n_tiles_a, 0, n_tiles_b - 1), 0)),
        ],
        out_specs=pl.BlockSpec(memory_space=pl.ANY),
        scratch_shapes=[
            pltpu.VMEM((2, stage_rows, dw), u_a.dtype),
            pltpu.SemaphoreType.DMA((2,)),
            pltpu.SemaphoreType.DMA,
            pltpu.SMEM((1,), jnp.int32),
        ],
        compiler_params=_cparams(("arbitrary",)),
        name="dispatch",
    )(lpos, meta, u_a, u_b)


def _moe_kernel(be_ref, br_ref, x_ref, wg_ref, wu_ref, wd_ref, bg_ref, bu_ref, bd_ref, o_ref, xb_scr, *, n_f):
    i = pl.program_id(0)
    f = pl.program_id(1)
    n_rows = br_ref[i]
    half = x_ref.shape[-1]

    @pl.when((n_rows > 0) & (f == 0))
    def _():
        lo, hi = _unpack_bf16_pairs(x_ref[...])
        xb_scr[:, :half] = lo
        xb_scr[:, half:] = hi
        o_ref[...] = jnp.broadcast_to(bd_ref[0], o_ref.shape)

    @pl.when(n_rows > 0)
    def _():
        wg = wg_ref[0].astype(BF16)
        wu = wu_ref[0].astype(BF16)
        wd = wd_ref[0].astype(BF16)
        rows = x_ref.shape[0] // MOE_ROW_CHUNKS
        for c in range(MOE_ROW_CHUNKS):
            sl = pl.ds(c * rows, rows)
            xb = xb_scr[sl, :]
            gate = jnp.dot(xb, wg, preferred_element_type=F32) + bg_ref[0]
            up = jnp.dot(xb, wu, preferred_element_type=F32) + bu_ref[0]
            gate = jnp.minimum(gate, SWIGLU_LIMIT)
            up = jnp.clip(up, -SWIGLU_LIMIT, SWIGLU_LIMIT)
            glu = gate * jax.nn.sigmoid(SWIGLU_ALPHA * gate)
            act = ((up + 1.0) * glu).astype(BF16)
            o_ref[sl, :] += jnp.dot(act, wd, preferred_element_type=F32)

    @pl.when((n_rows == 0) & (f == 0))
    def _():
        o_ref[...] = jnp.zeros(o_ref.shape, o_ref.dtype)


def _moe(xs, blk_e, blk_rows, w_gu, b_gu, w_dn, b_dn, tm):
    n_slots, dw = xs.shape
    n_blocks = n_slots // tm
    n_exp, d, two_ff = w_gu.shape
    d_ff = two_ff // 2
    tf = _pick(d_ff, MOE_TF)
    n_f = d_ff // tf

    def fe(f, br, i):
        return jnp.where(br[i] > 0, f, n_f - 1)

    grid_spec = pltpu.PrefetchScalarGridSpec(
        num_scalar_prefetch=2,
        grid=(n_blocks, n_f),
        in_specs=[
            pl.BlockSpec((tm, dw), lambda i, f, be, br: (jnp.where(br[i] > 0, i, 0), 0)),
            pl.BlockSpec((1, d, tf), lambda i, f, be, br: (be[i], 0, fe(f, br, i))),
            pl.BlockSpec((1, d, tf), lambda i, f, be, br: (be[i], 0, n_f + fe(f, br, i))),
            pl.BlockSpec((1, tf, d), lambda i, f, be, br: (be[i], fe(f, br, i), 0)),
            pl.BlockSpec((1, 1, tf), lambda i, f, be, br: (be[i], 0, fe(f, br, i))),
            pl.BlockSpec((1, 1, tf), lambda i, f, be, br: (be[i], 0, n_f + fe(f, br, i))),
            pl.BlockSpec((1, 1, d), lambda i, f, be, br: (be[i], 0, 0)),
        ],
        out_specs=pl.BlockSpec((tm, d), lambda i, f, be, br: (i, 0)),
        scratch_shapes=[pltpu.VMEM((tm, d), BF16)],
    )
    return pl.pallas_call(
        functools.partial(_moe_kernel, n_f=n_f),
        out_shape=jax.ShapeDtypeStruct((n_slots, d), F32),
        grid_spec=grid_spec,
        compiler_params=_cparams(("arbitrary", "arbitrary")),
        name="moe",
    )(blk_e, blk_rows, xs, w_gu, w_gu, w_dn,
      b_gu.reshape(n_exp, 1, two_ff), b_gu.reshape(n_exp, 1, two_ff), b_dn.reshape(n_exp, 1, d))


def _route(top_i, n_exp, tt, tm):
    n_tok = top_i.shape[0]
    n_tiles = n_tok // tt
    n_a = tt * TOP_K
    n_blocks = -(-(n_tok * TOP_K + n_tiles * n_exp * (SUBLANES - 1)) // tm) + n_exp
    e = top_i.reshape(n_tiles, n_a)
    onehot = (e[:, :, None] == jnp.arange(n_exp, dtype=jnp.int32)).astype(jnp.int32)
    csum = jnp.cumsum(onehot, axis=1)
    rank = jnp.sum((csum - 1) * onehot, axis=-1)
    cnt = csum[:, -1, :]
    seg = (cnt + SUBLANES - 1) // SUBLANES * SUBLANES
    loff = jnp.cumsum(seg, axis=1) - seg
    lpos = jnp.sum(onehot * loff[:, None, :], axis=-1) + rank
    group = jnp.sum(seg, axis=0)
    group_pad = (group + tm - 1) // tm * tm
    pad_end = jnp.cumsum(group_pad)
    base = pad_end - group_pad
    gbase = base[None, :] + jnp.cumsum(seg, axis=0) - seg
    zero_col = jnp.zeros((n_tiles, 1), jnp.int32)
    tile_meta = jnp.concatenate([loff, zero_col, gbase, zero_col, seg // SUBLANES, zero_col], axis=1)
    fill_meta = jnp.concatenate([
        jnp.zeros((n_exp + 1,), jnp.int32),
        base + group, pad_end[-1:],
        (group_pad - group) // SUBLANES, n_blocks - pad_end[-1:] // tm])
    meta = jnp.concatenate([tile_meta, fill_meta[None]], axis=0).astype(jnp.int32)
    blk_start = jnp.arange(n_blocks, dtype=jnp.int32) * tm
    blk_e = jnp.minimum(jnp.sum(blk_start[:, None] >= pad_end[None, :], axis=1), n_exp - 1).astype(jnp.int32)
    blk_rows = jnp.clip(base[blk_e] + group[blk_e] - blk_start, 0, tm).astype(jnp.int32)
    return (lpos.astype(jnp.int32).reshape(n_tiles, 1, n_a), meta.reshape(n_tiles + 1, 1, 3 * (n_exp + 1)),
            blk_e, blk_rows, n_blocks)


def _post2_kernel(lpos_ref, tg_ref, meta_ref, meta_nxt_ref, ys_hbm, x_ref, g2_ref, lng_ref, lnb_ref, o_ref,
                  stage_scr, sem, *, tt, n_exp, n_tiles, alpha):
    i = pl.program_id(0)
    slot = i % 2

    @pl.when(i == 0)
    def _():
        stage_scr[...] = jnp.zeros(stage_scr.shape, stage_scr.dtype)

    def chunk_copy(s, src_row, dst_row):
        return pltpu.make_async_copy(ys_hbm.at[pl.ds(src_row, SUBLANES), :],
                                     stage_scr.at[s, pl.ds(dst_row, SUBLANES), :], sem.at[s])

    n_seg = n_exp + 1

    def fetch(m_ref, s):
        for e in range(n_exp):
            loff, gbase, n_chunk = m_ref[0, 0, e], m_ref[0, 0, n_seg + e], m_ref[0, 0, 2 * n_seg + e]

            def start(c, _, loff=loff, gbase=gbase):
                chunk_copy(s, pl.multiple_of(gbase + c * SUBLANES, SUBLANES),
                           pl.multiple_of(loff + c * SUBLANES, SUBLANES)).start()
                return 0
            lax.fori_loop(0, n_chunk, start, 0)

    @pl.when(i == 0)
    def _():
        fetch(meta_ref, 0)

    @pl.when(i + 1 < n_tiles)
    def _():
        fetch(meta_nxt_ref, 1 - slot)

    total = meta_ref[0, 0, 2 * n_seg]
    for e in range(1, n_exp):
        total = total + meta_ref[0, 0, 2 * n_seg + e]

    def wait(c, _):
        chunk_copy(slot, 0, 0).wait()
        return 0
    lax.fori_loop(0, total, wait, 0)

    lpos = lpos_ref[...]
    tg = tg_ref[...]
    col = lax.broadcasted_iota(jnp.int32, (tt, stage_scr.shape[1]), 1)
    sel = jnp.where(col == lpos[:, 0:1], tg[:, 0:1], 0.0)
    for k in range(1, TOP_K):
        sel = sel + jnp.where(col == lpos[:, k:k + 1], tg[:, k:k + 1], 0.0)
    y = jnp.dot(sel.astype(BF16), stage_scr[slot].astype(BF16), preferred_element_type=F32)
    o_ref[...] = _layernorm(alpha * x_ref[...] + g2_ref[0] * y, lng_ref[...], lnb_ref[...])


def _post2(ys, lpos, top_g, meta, x1, mod, ln_g, ln_b, tok_per_mod, alpha, tile_off, tt, n_exp):
    n_tok, d = x1.shape
    n_tiles = n_tok // tt
    last_tile = meta.shape[0] - 2
    stage_rows = -(-(tt * TOP_K + n_exp * SUBLANES) // LANES) * LANES
    cur = lambda i: (tile_off + i, 0, 0)
    nxt = lambda i: (jnp.minimum(tile_off + i + 1, last_tile), 0, 0)
    smem = lambda a, fn: pl.BlockSpec((1, 1, a.shape[-1]), fn, memory_space=pltpu.SMEM)
    row = lambda a: pl.BlockSpec((1, a.shape[-1]), lambda i: (0, 0))
    return pl.pallas_call(
        functools.partial(_post2_kernel, tt=tt, n_exp=n_exp, n_tiles=n_tiles, alpha=alpha),
        out_shape=jax.ShapeDtypeStruct((n_tok, d), F32),
        grid=(n_tiles,),
        in_specs=[
            pl.BlockSpec((tt, TOP_K), lambda i: (tile_off + i, 0)),
            pl.BlockSpec((tt, LANES), lambda i: (i, 0)),
            smem(meta, cur), smem(meta, nxt),
            pl.BlockSpec(memory_space=pl.ANY),
            pl.BlockSpec((tt, d), lambda i: (i, 0)),
            pl.BlockSpec((1, 1, d), lambda i: ((i * tt) // tok_per_mod, 0, 5)),
            row(ln_g), row(ln_b),
        ],
        out_specs=pl.BlockSpec((tt, d), lambda i: (i, 0)),
        scratch_shapes=[
            pltpu.VMEM((2, stage_rows, d), F32),
            pltpu.SemaphoreType.DMA((2,)),
        ],
        compiler_params=_cparams(("arbitrary",)),
        name="post2",
    )(lpos, top_g, meta, meta, ys, x1, mod, ln_g, ln_b)


def _router_weights(w_router, b_router):
    d, n_exp = w_router.shape
    hi = w_router.astype(BF16)
    lo = (w_router - hi.astype(F32)).astype(BF16)
    pad = lambda a: jnp.pad(a, ((0, 0), (0, LANES - n_exp)))
    return jnp.concatenate([pad(hi), pad(lo)], axis=1), jnp.pad(b_router, (0, LANES - n_exp)).reshape(1, LANES)


def kernel(x_prompt, x_sample, cache_k, cache_v, state_h_fwd, state_h_bwd, c, c_ctx, w_ada, b_ada, w_in, b_in,
           conv_w, conv_b, rg_wa, rg_ba, rg_wx, rg_bx, rg_lam, rpb, w_rnn_out, w_na_out, w_o, b_o,
           ln1_g, ln1_b, ln2_g, ln2_b, w_router, b_router, w_gu, b_gu, w_dn, b_dn):
    depth, d, _ = w_in.shape
    bsz, seq, _ = x_prompt.shape
    dbsz, dseq, _ = x_sample.shape
    n_heads, hd = cache_k.shape[2], cache_k.shape[4]
    d_na = n_heads * hd
    d_rnn = conv_w.shape[2]
    n_exp = w_router.shape[2]
    alpha = (2 * depth) ** 0.25
    n_p, n_s = bsz * seq, dbsz * dseq

    xp = x_prompt.reshape(n_p, d)
    xs = x_sample.reshape(n_s, d)
    n_cond = -(-(1 + dbsz) // SUBLANES) * SUBLANES
    cond = jnp.concatenate([c_ctx[None], c, jnp.zeros((n_cond - 1 - dbsz, d), F32)], axis=0)
    ks, vs, hfs, hbs = [], [], [], []
    for l in range(depth):
        mod = _ada(cond, w_ada[l], b_ada[l]).reshape(n_cond, 1, 6 * d)
        mod_p, mod_s = mod[0:1], mod[1:1 + dbsz]
        w_in_l = w_in[l].astype(BF16)
        w_gates = jnp.concatenate([rg_wa[l], rg_wx[l]], axis=-1).astype(BF16)
        w_rnn_l, w_na_l, w_o_l = w_rnn_out[l].astype(BF16), w_na_out[l].astype(BF16), w_o[l].astype(BF16)
        w_rt, b_rt = _router_weights(w_router[l], b_router[l])
        row = lambda a: a[l].reshape(1, -1)

        def mixer(x, mod_g, n_b, t, h0, tok_per_mod, is_ctx):
            xr, qkv, g = _inproj(x, mod_g, w_in_l, b_in[l], d_rnn, d_na, tok_per_mod, F32 if is_ctx else BF16)
            yf, yb, h_last = _rglru(xr.reshape(n_b, t, d_rnn), conv_w[l], conv_b[l], w_gates,
                                    rg_ba[l], rg_bx[l], rg_lam[l], h0, is_ctx)
            qkv3 = qkv.reshape(n_b, t, 3 * d_na)
            if is_ctx:
                ya = _cattn(qkv3, n_heads, hd)
            else:
                ya = _nattn(qkv3, cache_k[:, l], cache_v[:, l], rpb[l], n_heads, hd)
            merged = _merge(yf.reshape(n_b * t, d_rnn), yb.reshape(n_b * t, d_rnn), ya.reshape(n_b * t, d_na),
                            g, w_rnn_l, w_na_l)
            x1, u2, ti, tg = _post1(merged, x, mod_g, w_o_l, row(b_o), row(ln1_g), row(ln1_b), w_rt, b_rt,
                                    tok_per_mod, alpha, n_exp)
            return x1, u2, ti, tg, qkv3, h_last

        x1p, u2p, tip, tgp, qkv_p, hl_p = mixer(xp, mod_p, bsz, seq, jnp.zeros((bsz, 2, d_rnn), F32), n_p, True)
        h0_s = jnp.stack([state_h_fwd[:, l], state_h_bwd[:, l]], axis=1)
        x1s, u2s, tis, tgs, _, _ = mixer(xs, mod_s, dbsz, dseq, h0_s, dseq, False)

        n_all = n_p + n_s
        tt = _pick(np.gcd(n_p, dseq), MOE_TT)
        tm = MOE_TM
        top_i = jnp.concatenate([tip[:, :TOP_K], tis[:, :TOP_K]], axis=0)
        lpos, meta, blk_e, blk_rows, n_blocks = _route(top_i, n_exp, tt, tm)
        xsort = _dispatch(u2p, u2s, lpos, meta, n_blocks * tm, tt, tm, n_exp)
        ys = _moe(xsort, blk_e, blk_rows, w_gu[l], b_gu[l], w_dn[l], b_dn[l], tm)
        lpos_tok = lpos.reshape(n_all, TOP_K)
        xp = _post2(ys, lpos_tok, tgp, meta, x1p, mod_p, row(ln2_g), row(ln2_b), n_p, alpha, 0, tt, n_exp)
        xs = _post2(ys, lpos_tok, tgs, meta, x1s, mod_s, row(ln2_g), row(ln2_b), dseq, alpha, n_p // tt, tt, n_exp)

        heads = lambda a: a.reshape(bsz, seq, n_heads, hd).transpose(0, 2, 1, 3)
        ks.append(heads(qkv_p[:, :, d_na:2 * d_na]))
        vs.append(heads(qkv_p[:, :, 2 * d_na:]))
        hfs.append(hl_p[:, 0])
        hbs.append(hl_p[:, 1])

    return (xp.reshape(bsz, seq, d), xs.reshape(dbsz, dseq, d),
            jnp.stack(ks, axis=1), jnp.stack(vs, axis=1), jnp.stack(hfs, axis=1), jnp.stack(hbs, axis=1))
```

```python
import functools

import numpy as np
import jax
import jax.numpy as jnp
from jax import lax
from jax.experimental import pallas as pl
from jax.experimental.pallas import tpu as pltpu

F32 = jnp.float32
BF16 = jnp.bfloat16

GRID_W = 64
WIN_ROWS = 8
WIN_COLS = 16
CONV_W = 4
RG_C = 8.0
TOP_K = 4
SWIGLU_LIMIT = 7.0
SWIGLU_ALPHA = 1.702
LN_EPS = 1e-5
NEG_INF = -1e30

VMEM_LIMIT_BYTES = 56 * 1024 * 1024
SUBLANES = 8
LANES = 128

NATTN_ROWS = 4
NATTN_SUB = 8
MOE_TT = 256
MOE_TM = 1024
MOE_TF = 256
MOE_ROW_CHUNKS = 2
ROW_CHUNKS = 2


def _cparams(sem):
    return pltpu.CompilerParams(dimension_semantics=sem, vmem_limit_bytes=VMEM_LIMIT_BYTES)


def _pick(n, pref):
    t = min(n, pref)
    while n % t:
        t //= 2
    return t


def _ada_kernel(c_ref, w_ref, b_ref, o_ref):
    c = c_ref[...]
    s = c * jax.nn.sigmoid(c)
    o_ref[...] = jnp.dot(s.astype(BF16), w_ref[...].astype(BF16), preferred_element_type=F32) + b_ref[...]


def _ada(cond, w_ada, b_ada):
    nb, d = cond.shape
    n = w_ada.shape[1]
    tn = _pick(n, 1024)
    return pl.pallas_call(
        _ada_kernel,
        out_shape=jax.ShapeDtypeStruct((nb, n), F32),
        grid=(n // tn,),
        in_specs=[
            pl.BlockSpec((nb, d), lambda j: (0, 0)),
            pl.BlockSpec((d, tn), lambda j: (0, j)),
            pl.BlockSpec((1, tn), lambda j: (0, j)),
        ],
        out_specs=pl.BlockSpec((nb, tn), lambda j: (0, j)),
        compiler_params=_cparams(("arbitrary",)),
        name="ada",
    )(cond, w_ada, b_ada.reshape(1, n))


def _inproj_kernel(x_ref, sh_ref, sc_ref, w_ref, b_ref, xr_ref, qkv_ref, g_ref, u_scr, *, n_xr, n_qkv):
    n = pl.program_id(1)

    @pl.when(n == 0)
    def _():
        u = x_ref[...] * (1.0 + sc_ref[0]) + sh_ref[0]
        u_scr[...] = u.astype(BF16)

    def project(o_ref):
        rows = u_scr.shape[0] // ROW_CHUNKS
        for c in range(ROW_CHUNKS):
            sl = pl.ds(c * rows, rows)
            acc = jnp.dot(u_scr[sl, :], w_ref[...], preferred_element_type=F32) + b_ref[...]
            o_ref[sl, :] = acc.astype(o_ref.dtype)

    @pl.when(n < n_xr)
    def _():
        project(xr_ref)

    @pl.when((n >= n_xr) & (n < n_xr + n_qkv))
    def _():
        project(qkv_ref)

    @pl.when(n >= n_xr + n_qkv)
    def _():
        project(g_ref)


def _inproj(x, mod, w_in, b_in, d_rnn, d_na, tok_per_mod, qkv_dtype):
    n_tok, d = x.shape
    d_in = w_in.shape[1]
    tn = _pick(np.gcd(np.gcd(d_rnn, d_na), d), 1024)
    tm = _pick(tok_per_mod, 1024 if jnp.dtype(qkv_dtype).itemsize == 2 else 512)
    n_xr, n_qkv, n_g = d_rnn // tn, 3 * d_na // tn, 2 * d // tn
    assert (n_xr + n_qkv + n_g) * tn == d_in

    def mod_map(col):
        return lambda i, j: ((i * tm) // tok_per_mod, 0, col)

    return pl.pallas_call(
        functools.partial(_inproj_kernel, n_xr=n_xr, n_qkv=n_qkv),
        out_shape=(
            jax.ShapeDtypeStruct((n_tok, d_rnn), F32),
            jax.ShapeDtypeStruct((n_tok, 3 * d_na), qkv_dtype),
            jax.ShapeDtypeStruct((n_tok, 2 * d), BF16),
        ),
        grid=(n_tok // tm, d_in // tn),
        in_specs=[
            pl.BlockSpec((tm, d), lambda i, j: (i, 0)),
            pl.BlockSpec((1, 1, d), mod_map(0)),
            pl.BlockSpec((1, 1, d), mod_map(1)),
            pl.BlockSpec((d, tn), lambda i, j: (0, j)),
            pl.BlockSpec((1, tn), lambda i, j: (0, j)),
        ],
        out_specs=(
            pl.BlockSpec((tm, tn), lambda i, j: (i, jnp.minimum(j, n_xr - 1))),
            pl.BlockSpec((tm, tn), lambda i, j: (i, jnp.clip(j - n_xr, 0, n_qkv - 1))),
            pl.BlockSpec((tm, tn), lambda i, j: (i, jnp.clip(j - n_xr - n_qkv, 0, n_g - 1))),
        ),
        scratch_shapes=[pltpu.VMEM((tm, d), BF16)],
        compiler_params=_cparams(("arbitrary", "arbitrary")),
        name="inproj",
    )(x, mod, mod, w_in, b_in.reshape(1, d_in))


def _rglru_kernel(xf_ref, xfp_ref, xfn_ref, xb_ref, xbp_ref, xbn_ref,
                  cw_ref, cb_ref, w_ref, ba_ref, bx_ref, lam_ref, h0_ref,
                  yf_ref, yb_ref, hl_ref,
                  xext_scr, a_scr, b_scr, carry_scr, *, t_tile, n_t, seq_len, reset):
    j = pl.program_id(2)
    c = xf_ref.shape[-1]
    n_grp = t_tile // SUBLANES

    @pl.when(j == 0)
    def _():
        carry_scr[...] = h0_ref[0]

    row = lax.broadcasted_iota(jnp.int32, (t_tile, 1), 0)
    sub3 = lax.broadcasted_iota(jnp.int32, (1, SUBLANES, 1), 1)

    def direction(d, x_ref, xp_ref, xn_ref, y_ref):
        jt = j if d == 0 else n_t - 1 - j
        xext_scr[0:SUBLANES, :] = jnp.where(jt > 0, xp_ref[0], 0.0)
        xext_scr[SUBLANES:SUBLANES + t_tile, :] = x_ref[0]
        xext_scr[SUBLANES + t_tile:2 * SUBLANES + t_tile, :] = jnp.where(jt < n_t - 1, xn_ref[0], 0.0)
        left = CONV_W // 2
        xc = cb_ref[...] + xext_scr[pl.ds(SUBLANES - left, t_tile), :] * cw_ref[0:1, :]
        for k in range(1, CONV_W):
            xc = xc + xext_scr[pl.ds(SUBLANES - left + k, t_tile), :] * cw_ref[k:k + 1, :]
        pre = jnp.dot(xc.astype(BF16), w_ref[d, 0], preferred_element_type=F32)
        r_gate = jax.nn.sigmoid(pre[:, :c] + ba_ref[d:d + 1, :])
        i_gate = jax.nn.sigmoid(pre[:, c:] + bx_ref[d:d + 1, :])
        lam = lam_ref[d:d + 1, :]
        softplus_neg_lam = jnp.maximum(-lam, 0.0) + jnp.log1p(jnp.exp(-jnp.abs(lam)))
        log_a = -RG_C * r_gate * softplus_neg_lam
        a = jnp.exp(log_a)
        om = 1.0 - a * a
        mult = jnp.where(om > 0.0, om * lax.rsqrt(om), 0.0)
        if reset:
            first = 0 if d == 0 else seq_len - 1
            mult = jnp.where(jt * t_tile + row == first, 1.0, mult)
        b = mult * i_gate * xc
        a = a.reshape(n_grp, SUBLANES, c)
        b = b.reshape(n_grp, SUBLANES, c)
        for s in (1, 2, 4):
            if d == 0:
                a_sh = pltpu.roll(a, s, axis=1)
                b_sh = pltpu.roll(b, s, axis=1)
                m = sub3 >= s
            else:
                a_sh = pltpu.roll(a, SUBLANES - s, axis=1)
                b_sh = pltpu.roll(b, SUBLANES - s, axis=1)
                m = sub3 < SUBLANES - s
            b = jnp.where(m, b + a * b_sh, b)
            a = jnp.where(m, a * a_sh, a)
        a_scr[...] = a.reshape(t_tile, c)
        b_scr[...] = b.reshape(t_tile, c)

        def body(g, h):
            gg = g if d == 0 else n_grp - 1 - g
            off = pl.multiple_of(gg * SUBLANES, SUBLANES)
            h_rows = b_scr[pl.ds(off, SUBLANES), :] + a_scr[pl.ds(off, SUBLANES), :] * h
            y_ref[0, pl.ds(off, SUBLANES), :] = h_rows.astype(y_ref.dtype)
            return h_rows[SUBLANES - 1:SUBLANES, :] if d == 0 else h_rows[0:1, :]

        h = lax.fori_loop(0, n_grp, body, carry_scr[d:d + 1, :], unroll=4)
        carry_scr[d:d + 1, :] = h
        hl_ref[0, d:d + 1, :] = h

    direction(0, xf_ref, xfp_ref, xfn_ref, yf_ref)
    direction(1, xb_ref, xbp_ref, xbn_ref, yb_ref)


def _rglru(xr, conv_w, conv_b, w_gates, rg_ba, rg_bx, rg_lam, h0, reset):
    bsz, t, d_rnn = xr.shape
    n_blk, c = w_gates.shape[1], w_gates.shape[2]
    t_tile = _pick(t, 1024)
    n_t = t // t_tile
    tb = t_tile // SUBLANES
    last_blk = t // SUBLANES - 1

    def main(rev):
        return lambda b, n, j: (b, (n_t - 1 - j) if rev else j, n)

    def prev(rev):
        return lambda b, n, j: (b, jnp.maximum(((n_t - 1 - j) if rev else j) * tb - 1, 0), n)

    def nxt(rev):
        return lambda b, n, j: (b, jnp.minimum((((n_t - 1 - j) if rev else j) + 1) * tb, last_blk), n)

    vec = lambda rows: pl.BlockSpec((rows, c), lambda b, n, j: (0, n))
    return pl.pallas_call(
        functools.partial(_rglru_kernel, t_tile=t_tile, n_t=n_t, seq_len=t, reset=reset),
        out_shape=(
            jax.ShapeDtypeStruct((bsz, t, d_rnn), BF16),
            jax.ShapeDtypeStruct((bsz, t, d_rnn), BF16),
            jax.ShapeDtypeStruct((bsz, 2, d_rnn), F32),
        ),
        grid=(bsz, n_blk, n_t),
        in_specs=[
            pl.BlockSpec((1, t_tile, c), main(False)),
            pl.BlockSpec((1, SUBLANES, c), prev(False)),
            pl.BlockSpec((1, SUBLANES, c), nxt(False)),
            pl.BlockSpec((1, t_tile, c), main(True)),
            pl.BlockSpec((1, SUBLANES, c), prev(True)),
            pl.BlockSpec((1, SUBLANES, c), nxt(True)),
            vec(CONV_W),
            vec(1),
            pl.BlockSpec((2, 1, c, 2 * c), lambda b, n, j: (0, n, 0, 0)),
            vec(2), vec(2), vec(2),
            pl.BlockSpec((1, 2, c), lambda b, n, j: (b, 0, n)),
        ],
        out_specs=(
            pl.BlockSpec((1, t_tile, c), main(False)),
            pl.BlockSpec((1, t_tile, c), main(True)),
            pl.BlockSpec((1, 2, c), lambda b, n, j: (b, 0, n)),
        ),
        scratch_shapes=[
            pltpu.VMEM((t_tile + 2 * SUBLANES, c), F32),
            pltpu.VMEM((t_tile, c), F32),
            pltpu.VMEM((t_tile, c), F32),
            pltpu.VMEM((2, c), F32),
        ],
        compiler_params=_cparams(("arbitrary", "arbitrary", "arbitrary")),
        name="rglru",
    )(xr, xr, xr, xr, xr, xr, conv_w, conv_b.reshape(1, d_rnn), w_gates, rg_ba, rg_bx, rg_lam, h0)


def _qk(q, k):
    return lax.dot_general(q, k, (((1,), (1,)), ((), ())), preferred_element_type=F32)


def _cattn_kernel(q_ref, k_ref, v_ref, o_ref, *, n_heads, hd, scale):
    for h in range(n_heads):
        sl = slice(h * hd, (h + 1) * hd)
        q = q_ref[0, :, sl].astype(BF16)
        k = k_ref[0, :, sl].astype(BF16)
        v = v_ref[0, :, sl].astype(BF16)
        s = _qk(q, k) * scale
        p = jnp.exp(s - jnp.max(s, axis=-1, keepdims=True))
        l = jnp.sum(p, axis=-1, keepdims=True)
        o = jnp.dot(p.astype(BF16), v, preferred_element_type=F32) / l
        o_ref[0, :, sl] = o.astype(o_ref.dtype)


def _cattn(qkv, n_heads, hd):
    bsz, s, _ = qkv.shape
    d_na = n_heads * hd
    spec = lambda col: pl.BlockSpec((1, s, d_na), lambda b: (b, 0, col))
    return pl.pallas_call(
        functools.partial(_cattn_kernel, n_heads=n_heads, hd=hd, scale=hd ** -0.5),
        out_shape=jax.ShapeDtypeStruct((bsz, s, d_na), BF16),
        grid=(bsz,),
        in_specs=[spec(0), spec(1), spec(2)],
        out_specs=pl.BlockSpec((1, s, d_na), lambda b: (b, 0, 0)),
        compiler_params=_cparams(("arbitrary",)),
        name="cattn",
    )(qkv, qkv, qkv)


def _nattn_window_start(rb, rows, r_blk, w_blk):
    return jnp.clip(rb * r_blk - WIN_ROWS // 2, 0, rows - w_blk)


def _nattn_kernel(q_ref, k_ref, v_ref, ck_ref, cv_ref, bias_ref, o_ref, *, hd, scale, rows, r_blk, w_blk, n_sub):
    step = pl.program_id(1)
    n_rb = rows // r_blk
    tq = r_blk * GRID_W
    for s in range(n_sub):
        rb = step * n_sub + s
        pat = jnp.where(rb == 0, 0, jnp.where(rb == n_rb - 1, 2, 1))
        w0 = _nattn_window_start(rb, rows, r_blk, w_blk)
        start = pl.multiple_of(w0 * GRID_W, GRID_W)
        k_win = k_ref[0, pl.ds(start, w_blk * GRID_W), :]
        v_win = v_ref[0, pl.ds(start, w_blk * GRID_W), :]
        q = q_ref[0, s * tq:(s + 1) * tq, :]
        outs = []
        for h in range(q.shape[-1] // hd):
            sl = slice(h * hd, (h + 1) * hd)
            qh = q[:, sl]
            s_lat = _qk(qh, k_win[:, sl]) * scale + bias_ref[h, pat]
            s_ctx = _qk(qh, ck_ref[0, h].astype(BF16)) * scale
            m = jnp.maximum(jnp.max(s_lat, axis=-1, keepdims=True), jnp.max(s_ctx, axis=-1, keepdims=True))
            p_lat = jnp.exp(s_lat - m)
            p_ctx = jnp.exp(s_ctx - m)
            l = jnp.sum(p_lat, axis=-1, keepdims=True) + jnp.sum(p_ctx, axis=-1, keepdims=True)
            o = (jnp.dot(p_lat.astype(BF16), v_win[:, sl], preferred_element_type=F32)
                 + jnp.dot(p_ctx.astype(BF16), cv_ref[0, h].astype(BF16), preferred_element_type=F32))
            outs.append(o / l)
        o_ref[0, s * tq:(s + 1) * tq, :] = jnp.concatenate(outs, axis=-1).astype(o_ref.dtype)


def _nattn_bias(rpb, rows, r_blk, w_blk):
    n_rb = rows // r_blk
    wr = min(WIN_ROWS, rows)
    n_heads = rpb.shape[0]
    col = np.arange(GRID_W)
    cstart = np.clip(col - WIN_COLS // 2, 0, GRID_W - WIN_COLS)
    col_ok = (col[None, :] >= cstart[:, None]) & (col[None, :] < cstart[:, None] + WIN_COLS)
    rp = jnp.pad(rpb.astype(F32), ((0, 0), (0, 0), (GRID_W, GRID_W)))
    shifted = [rp[:, :, GRID_W + WIN_COLS - 1 - qc:2 * GRID_W + WIN_COLS - 1 - qc] for qc in range(GRID_W)]
    tiles = jnp.where(col_ok[None, None], jnp.stack(shifted, axis=2), NEG_INF)
    masked = jnp.full((n_heads, GRID_W, GRID_W), NEG_INF, F32)
    pats = []
    for rb in (0, min(1, n_rb - 1), n_rb - 1):
        r0 = rb * r_blk
        w0 = int(np.clip(r0 - WIN_ROWS // 2, 0, rows - w_blk))
        strips = []
        for i in range(r_blk):
            qrow = r0 + i
            rstart = int(np.clip(qrow - wr // 2, 0, rows - wr))
            strip = []
            for jj in range(w_blk):
                krow = w0 + jj
                ok = rstart <= krow < rstart + wr
                strip.append(tiles[:, krow - qrow + WIN_ROWS - 1] if ok else masked)
            strips.append(jnp.concatenate(strip, axis=-1))
        pats.append(jnp.concatenate(strips, axis=-2))
    return jnp.stack(pats, axis=1)


def _nattn(qkv, ck, cv, rpb, n_heads, hd):
    bsz, t, _ = qkv.shape
    d_na = n_heads * hd
    rows = t // GRID_W
    r_blk = min(NATTN_ROWS, rows)
    w_blk = min(r_blk + WIN_ROWS - 1, rows)
    n_rb = rows // r_blk
    for rb in range(1, n_rb - 1):
        assert 0 <= rb * r_blk - WIN_ROWS // 2 <= rows - w_blk
    hp = LANES // hd
    n_pair = d_na // LANES
    past = ck.shape[2]
    bias = _nattn_bias(rpb, rows, r_blk, w_blk)
    n_sub = _pick(n_rb, NATTN_SUB)
    tq, tk = n_sub * r_blk * GRID_W, w_blk * GRID_W

    return pl.pallas_call(
        functools.partial(_nattn_kernel, hd=hd, scale=hd ** -0.5, rows=rows, r_blk=r_blk, w_blk=w_blk,
                          n_sub=n_sub),
        out_shape=jax.ShapeDtypeStruct((bsz, t, d_na), BF16),
        grid=(n_pair, n_rb // n_sub, bsz),
        in_specs=[
            pl.BlockSpec((1, tq, LANES), lambda p, rb, b: (b, rb, p)),
            pl.BlockSpec((1, t, LANES), lambda p, rb, b: (b, 0, n_pair + p)),
            pl.BlockSpec((1, t, LANES), lambda p, rb, b: (b, 0, 2 * n_pair + p)),
            pl.BlockSpec((1, hp, past, hd), lambda p, rb, b: (b, p, 0, 0)),
            pl.BlockSpec((1, hp, past, hd), lambda p, rb, b: (b, p, 0, 0)),
            pl.BlockSpec((hp, 3, r_blk * GRID_W, tk), lambda p, rb, b: (p, 0, 0, 0)),
        ],
        out_specs=pl.BlockSpec((1, tq, LANES), lambda p, rb, b: (b, rb, p)),
        compiler_params=_cparams(("arbitrary", "arbitrary", "arbitrary")),
        name="nattn",
    )(qkv, qkv, qkv, ck, cv, bias)


def _merge_kernel(yf_ref, yb_ref, ya_ref, ga_ref, gb_ref, wr_ref, wa_ref, o_ref):
    rows = o_ref.shape[0] // ROW_CHUNKS
    for c in range(ROW_CHUNKS):
        sl = pl.ds(c * rows, rows)
        y_rnn = (yf_ref[sl, :].astype(F32) + yb_ref[sl, :].astype(F32)).astype(BF16)
        t_rnn = jnp.dot(y_rnn, wr_ref[...], preferred_element_type=F32)
        t_na = jnp.dot(ya_ref[sl, :], wa_ref[...], preferred_element_type=F32)
        merged = (jax.nn.sigmoid(ga_ref[sl, :].astype(F32)) * t_rnn
                  + jax.nn.sigmoid(gb_ref[sl, :].astype(F32)) * t_na)
        o_ref[sl, :] = merged.astype(o_ref.dtype)


def _merge(yf, yb, ya, g, w_rnn_out, w_na_out):
    n_tok, d_rnn = yf.shape
    d_na = ya.shape[1]
    d = w_rnn_out.shape[1]
    tm = _pick(n_tok, 512)
    tn = _pick(d, 1024)
    n_n = d // tn
    return pl.pallas_call(
        _merge_kernel,
        out_shape=jax.ShapeDtypeStruct((n_tok, d), BF16),
        grid=(n_n, n_tok // tm),
        in_specs=[
            pl.BlockSpec((tm, d_rnn), lambda j, i: (i, 0)),
            pl.BlockSpec((tm, d_rnn), lambda j, i: (i, 0)),
            pl.BlockSpec((tm, d_na), lambda j, i: (i, 0)),
            pl.BlockSpec((tm, tn), lambda j, i: (i, j)),
            pl.BlockSpec((tm, tn), lambda j, i: (i, n_n + j)),
            pl.BlockSpec((d_rnn, tn), lambda j, i: (0, j)),
            pl.BlockSpec((d_na, tn), lambda j, i: (0, j)),
        ],
        out_specs=pl.BlockSpec((tm, tn), lambda j, i: (i, j)),
        compiler_params=_cparams(("arbitrary", "arbitrary")),
        name="merge",
    )(yf, yb, ya, g, g, w_rnn_out, w_na_out)


def _layernorm(z, g, b):
    mu = jnp.mean(z, axis=-1, keepdims=True)
    zc = z - mu
    var = jnp.mean(zc * zc, axis=-1, keepdims=True)
    return zc * lax.rsqrt(var + LN_EPS) * g + b


def _pack_bf16_pairs(x):
    half = x.shape[-1] // 2
    lo = lax.bitcast_convert_type(x[:, :half].astype(BF16).astype(F32), jnp.uint32)
    hi = lax.bitcast_convert_type(x[:, half:].astype(BF16).astype(F32), jnp.uint32)
    return (lo >> 16) | (hi & jnp.uint32(0xFFFF0000))


def _unpack_bf16_pairs(w):
    lo = lax.bitcast_convert_type(w << 16, F32).astype(BF16)
    hi = lax.bitcast_convert_type(w & jnp.uint32(0xFFFF0000), F32).astype(BF16)
    return lo, hi


def _post1_kernel(m_ref, x_ref, g1_ref, sh2_ref, sc2_ref, wo_ref, bo_ref, lng_ref, lnb_ref, wrt_ref, brt_ref,
                  x1_ref, u2_ref, ti_ref, tg_ref, *, alpha, n_experts):
    rows = x_ref.shape[0] // ROW_CHUNKS
    for c in range(ROW_CHUNKS):
        _post1_rows(pl.ds(c * rows, rows), m_ref, x_ref, g1_ref, sh2_ref, sc2_ref, wo_ref, bo_ref, lng_ref, lnb_ref,
                    wrt_ref, brt_ref, x1_ref, u2_ref, ti_ref, tg_ref, alpha, n_experts)


def _post1_rows(sl, m_ref, x_ref, g1_ref, sh2_ref, sc2_ref, wo_ref, bo_ref, lng_ref, lnb_ref, wrt_ref, brt_ref,
                x1_ref, u2_ref, ti_ref, tg_ref, alpha, n_experts):
    mix = jnp.dot(m_ref[sl, :], wo_ref[...], preferred_element_type=F32) + bo_ref[...]
    x1 = _layernorm(alpha * x_ref[sl, :] + g1_ref[0] * mix, lng_ref[...], lnb_ref[...])
    x1_ref[sl, :] = x1
    u2 = x1 * (1.0 + sc2_ref[0]) + sh2_ref[0]
    u2_ref[sl, :] = _pack_bf16_pairs(u2)
    hi = u2.astype(BF16)
    lo = (u2 - hi.astype(F32)).astype(BF16)
    ph = jnp.dot(hi, wrt_ref[...], preferred_element_type=F32)
    plo = jnp.dot(lo, wrt_ref[...], preferred_element_type=F32)
    logits = (ph[:, :LANES] + ph[:, LANES:]) + (plo[:, :LANES] + plo[:, LANES:]) + brt_ref[...]
    lane = lax.broadcasted_iota(jnp.int32, logits.shape, 1)
    lane_f = lane.astype(F32)
    logits = jnp.where(lane < n_experts, logits, NEG_INF)
    top_i = jnp.zeros(logits.shape, jnp.int32)
    top_e = jnp.zeros(logits.shape, F32)
    v0 = None
    denom = None
    for k in range(TOP_K):
        v = jnp.max(logits, axis=-1, keepdims=True)
        idx = jnp.min(jnp.where(logits == v, lane_f, float(LANES)), axis=-1, keepdims=True).astype(jnp.int32)
        if k == 0:
            v0 = v
        e = jnp.exp(v - v0)
        denom = e if k == 0 else denom + e
        top_i = jnp.where(lane == k, idx, top_i)
        top_e = jnp.where(lane == k, e, top_e)
        logits = jnp.where(lane == idx, NEG_INF, logits)
    ti_ref[sl, :] = top_i
    tg_ref[sl, :] = top_e / denom


def _post1(merged, x, mod, w_o, b_o, ln_g, ln_b, w_rt, b_rt, tok_per_mod, alpha, n_experts):
    n_tok, d = x.shape
    tm = _pick(tok_per_mod, 512)

    def mod_map(col):
        return lambda i: ((i * tm) // tok_per_mod, 0, col)

    row = lambda a: pl.BlockSpec((1, a.shape[-1]), lambda i: (0, 0))
    tile = pl.BlockSpec((tm, d), lambda i: (i, 0))
    small = pl.BlockSpec((tm, LANES), lambda i: (i, 0))
    return pl.pallas_call(
        functools.partial(_post1_kernel, alpha=alpha, n_experts=n_experts),
        out_shape=(
            jax.ShapeDtypeStruct((n_tok, d), F32),
            jax.ShapeDtypeStruct((n_tok, d // 2), jnp.uint32),
            jax.ShapeDtypeStruct((n_tok, LANES), jnp.int32),
            jax.ShapeDtypeStruct((n_tok, LANES), F32),
        ),
        grid=(n_tok // tm,),
        in_specs=[
            tile, tile,
            pl.BlockSpec((1, 1, d), mod_map(2)),
            pl.BlockSpec((1, 1, d), mod_map(3)),
            pl.BlockSpec((1, 1, d), mod_map(4)),
            pl.BlockSpec((d, d), lambda i: (0, 0)),
            row(b_o), row(ln_g), row(ln_b),
            pl.BlockSpec((d, 2 * LANES), lambda i: (0, 0)),
            row(b_rt),
        ],
        out_specs=(tile, pl.BlockSpec((tm, d // 2), lambda i: (i, 0)), small, small),
        compiler_params=_cparams(("arbitrary",)),
        name="post1",
    )(merged, x, mod, mod, mod, w_o, b_o, ln_g, ln_b, w_rt, b_rt)


def _dispatch_kernel(lpos_ref, meta_ref, xa_ref, xb_ref, xs_hbm, stage_scr, sem, bsem, nprev_scr,
                     *, tt, tm, n_exp, n_tiles_a, n_tiles):
    i = pl.program_id(0)
    slot = i % 2
    n_seg = n_exp + 1
    is_fill = i == n_tiles

    @pl.when(i == 0)
    def _():
        stage_scr[...] = jnp.zeros(stage_scr.shape, stage_scr.dtype)

    @pl.when(is_fill)
    def _():
        stage_scr[slot] = jnp.zeros(stage_scr.shape[1:], stage_scr.dtype)

    def place_from(x_ref):
        def place(t, _):
            row = x_ref[pl.ds(t, 1), :]
            for k in range(TOP_K):
                stage_scr[slot, pl.ds(lpos_ref[0, 0, t * TOP_K + k], 1), :] = row
            return 0
        lax.fori_loop(0, tt, place, 0, unroll=2)

    @pl.when(i < n_tiles_a)
    def _():
        place_from(xa_ref)

    @pl.when((i >= n_tiles_a) & jnp.logical_not(is_fill))
    def _():
        place_from(xb_ref)

    def chunk_copy(s, src_row, dst_row):
        return pltpu.make_async_copy(stage_scr.at[s, pl.ds(src_row, SUBLANES), :],
                                     xs_hbm.at[pl.ds(dst_row, SUBLANES), :], sem.at[s])

    def block_copy(dst_row):
        return pltpu.make_async_copy(stage_scr.at[slot, pl.ds(0, tm), :], xs_hbm.at[pl.ds(dst_row, tm), :], bsem)

    src_step = jnp.where(is_fill, 0, SUBLANES)
    total = 0
    for e in range(n_exp):
        loff, gbase, n_chunk = meta_ref[0, 0, e], meta_ref[0, 0, n_seg + e], meta_ref[0, 0, 2 * n_seg + e]

        def start(c, _, loff=loff, gbase=gbase):
            chunk_copy(slot, pl.multiple_of(loff + c * src_step, SUBLANES),
                       pl.multiple_of(gbase + c * SUBLANES, SUBLANES)).start()
            return 0
        lax.fori_loop(0, n_chunk, start, 0)
        total = total + n_chunk

    def drain(s, n):
        def wait(c, _):
            chunk_copy(s, 0, 0).wait()
            return 0
        lax.fori_loop(0, n, wait, 0)

    @pl.when(i > 0)
    def _():
        drain(1 - slot, nprev_scr[0])
    nprev_scr[0] = total

    @pl.when(is_fill)
    def _():
        first_row, n_blk = meta_ref[0, 0, n_seg + n_exp], meta_ref[0, 0, 2 * n_seg + n_exp]

        def start(b, _):
            block_copy(pl.multiple_of(first_row + b * tm, tm)).start()
            return 0
        lax.fori_loop(0, n_blk, start, 0)

        def wait(b, _):
            block_copy(0).wait()
            return 0
        lax.fori_loop(0, n_blk, wait, 0)
        drain(slot, total)


def _dispatch(u_a, u_b, lpos, meta, n_slots, tt, tm, n_exp):
    dw = u_a.shape[1]
    n_tiles_a, n_tiles_b = u_a.shape[0] // tt, u_b.shape[0] // tt
    n_tiles = n_tiles_a + n_tiles_b
    stage_rows = max(tt * TOP_K + n_exp * SUBLANES, tm)
    return pl.pallas_call(
        functools.partial(_dispatch_kernel, tt=tt, tm=tm, n_exp=n_exp, n_tiles_a=n_tiles_a, n_tiles=n_tiles),
        out_shape=jax.ShapeDtypeStruct((n_slots, dw), u_a.dtype),
        grid=(n_tiles + 1,),
        in_specs=[
            pl.BlockSpec((1, 1, lpos.shape[-1]), lambda i: (jnp.minimum(i, n_tiles - 1), 0, 0),
                         memory_space=pltpu.SMEM),
            pl.BlockSpec((1, 1, meta.shape[-1]), lambda i: (i, 0, 0), memory_space=pltpu.SMEM),
            pl.BlockSpec((tt, dw), lambda i: (jnp.minimum(i, n_tiles_a - 1), 0)),
            pl.BlockSpec((tt, dw), lambda i: (jnp.clip(i - n_tiles_a, 0, n_tiles_b - 1), 0)),
        ],
        out_specs=pl.BlockSpec(memory_space=pl.ANY),
        scratch_shapes=[
            pltpu.VMEM((2, stage_rows, dw), u_a.dtype),
            pltpu.SemaphoreType.DMA((2,)),
            pltpu.SemaphoreType.DMA,
            pltpu.SMEM((1,), jnp.int32),
        ],
        compiler_params=_cparams(("arbitrary",)),
        name="dispatch",
    )(lpos, meta, u_a, u_b)


def _moe_kernel(be_ref, br_ref, x_ref, wg_ref, wu_ref, wd_ref, bg_ref, bu_ref, bd_ref, o_ref, xb_scr, *, n_f):
    i = pl.program_id(0)
    f = pl.program_id(1)
    n_rows = br_ref[i]
    half = x_ref.shape[-1]

    @pl.when((n_rows > 0) & (f == 0))
    def _():
        lo, hi = _unpack_bf16_pairs(x_ref[...])
        xb_scr[:, :half] = lo
        xb_scr[:, half:] = hi
        o_ref[...] = jnp.broadcast_to(bd_ref[0], o_ref.shape)

    @pl.when(n_rows > 0)
    def _():
        wg = wg_ref[0].astype(BF16)
        wu = wu_ref[0].astype(BF16)
        wd = wd_ref[0].astype(BF16)
        rows = x_ref.shape[0] // MOE_ROW_CHUNKS
        for c in range(MOE_ROW_CHUNKS):
            sl = pl.ds(c * rows, rows)
            xb = xb_scr[sl, :]
            gate = jnp.dot(xb, wg, preferred_element_type=F32) + bg_ref[0]
            up = jnp.dot(xb, wu, preferred_element_type=F32) + bu_ref[0]
            gate = jnp.minimum(gate, SWIGLU_LIMIT)
            up = jnp.clip(up, -SWIGLU_LIMIT, SWIGLU_LIMIT)
            glu = gate * jax.nn.sigmoid(SWIGLU_ALPHA * gate)
            act = ((up + 1.0) * glu).astype(BF16)
            o_ref[sl, :] += jnp.dot(act, wd, preferred_element_type=F32)

    @pl.when((n_rows == 0) & (f == 0))
    def _():
        o_ref[...] = jnp.zeros(o_ref.shape, o_ref.dtype)


def _moe(xs, blk_e, blk_rows, w_gu, b_gu, w_dn, b_dn, tm):
    n_slots, dw = xs.shape
    n_blocks = n_slots // tm
    n_exp, d, two_ff = w_gu.shape
    d_ff = two_ff // 2
    tf = _pick(d_ff, MOE_TF)
    n_f = d_ff // tf

    def fe(f, br, i):
        return jnp.where(br[i] > 0, f, n_f - 1)

    grid_spec = pltpu.PrefetchScalarGridSpec(
        num_scalar_prefetch=2,
        grid=(n_blocks, n_f),
        in_specs=[
            pl.BlockSpec((tm, dw), lambda i, f, be, br: (jnp.where(br[i] > 0, i, 0), 0)),
            pl.BlockSpec((1, d, tf), lambda i, f, be, br: (be[i], 0, fe(f, br, i))),
            pl.BlockSpec((1, d, tf), lambda i, f, be, br: (be[i], 0, n_f + fe(f, br, i))),
            pl.BlockSpec((1, tf, d), lambda i, f, be, br: (be[i], fe(f, br, i), 0)),
            pl.BlockSpec((1, 1, tf), lambda i, f, be, br: (be[i], 0, fe(f, br, i))),
            pl.BlockSpec((1, 1, tf), lambda i, f, be, br: (be[i], 0, n_f + fe(f, br, i))),
            pl.BlockSpec((1, 1, d), lambda i, f, be, br: (be[i], 0, 0)),
        ],
        out_specs=pl.BlockSpec((tm, d), lambda i, f, be, br: (i, 0)),
        scratch_shapes=[pltpu.VMEM((tm, d), BF16)],
    )
    return pl.pallas_call(
        functools.partial(_moe_kernel, n_f=n_f),
        out_shape=jax.ShapeDtypeStruct((n_slots, d), F32),
        grid_spec=grid_spec,
        compiler_params=_cparams(("arbitrary", "arbitrary")),
        name="moe",
    )(blk_e, blk_rows, xs, w_gu, w_gu, w_dn,
      b_gu.reshape(n_exp, 1, two_ff), b_gu.reshape(n_exp, 1, two_ff), b_dn.reshape(n_exp, 1, d))


def _route(top_i, n_exp, tt, tm):
    n_tok = top_i.shape[0]
    n_tiles = n_tok // tt
    n_a = tt * TOP_K
    n_blocks = -(-(n_tok * TOP_K + n_tiles * n_exp * (SUBLANES - 1)) // tm) + n_exp
    e = top_i.reshape(n_tiles, n_a)
    onehot = (e[:, :, None] == jnp.arange(n_exp, dtype=jnp.int32)).astype(jnp.int32)
    csum = jnp.cumsum(onehot, axis=1)
    rank = jnp.sum((csum - 1) * onehot, axis=-1)
    cnt = csum[:, -1, :]
    seg = (cnt + SUBLANES - 1) // SUBLANES * SUBLANES
    loff = jnp.cumsum(seg, axis=1) - seg
    lpos = jnp.sum(onehot * loff[:, None, :], axis=-1) + rank
    group = jnp.sum(seg, axis=0)
    group_pad = (group + tm - 1) // tm * tm
    pad_end = jnp.cumsum(group_pad)
    base = pad_end - group_pad
    gbase = base[None, :] + jnp.cumsum(seg, axis=0) - seg
    zero_col = jnp.zeros((n_tiles, 1), jnp.int32)
    tile_meta = jnp.concatenate([loff, zero_col, gbase, zero_col, seg // SUBLANES, zero_col], axis=1)
    fill_meta = jnp.concatenate([
        jnp.zeros((n_exp + 1,), jnp.int32),
        base + group, pad_end[-1:],
        (group_pad - group) // SUBLANES, n_blocks - pad_end[-1:] // tm])
    meta = jnp.concatenate([tile_meta, fill_meta[None]], axis=0).astype(jnp.int32)
    blk_start = jnp.arange(n_blocks, dtype=jnp.int32) * tm
    blk_e = jnp.minimum(jnp.sum(blk_start[:, None] >= pad_end[None, :], axis=1), n_exp - 1).astype(jnp.int32)
    blk_rows = jnp.clip(base[blk_e] + group[blk_e] - blk_start, 0, tm).astype(jnp.int32)
    return (lpos.astype(jnp.int32).reshape(n_tiles, 1, n_a), meta.reshape(n_tiles + 1, 1, 3 * (n_exp + 1)),
            blk_e, blk_rows, n_blocks)


def _post2_kernel(lpos_ref, tg_ref, meta_ref, meta_nxt_ref, ys_hbm, x_ref, g2_ref, lng_ref, lnb_ref, o_ref,
                  stage_scr, sem, *, tt, n_exp, n_tiles, alpha):
    i = pl.program_id(0)
    slot = i % 2

    @pl.when(i == 0)
    def _():
        stage_scr[...] = jnp.zeros(stage_scr.shape, stage_scr.dtype)

    def chunk_copy(s, src_row, dst_row):
        return pltpu.make_async_copy(ys_hbm.at[pl.ds(src_row, SUBLANES), :],
                                     stage_scr.at[s, pl.ds(dst_row, SUBLANES), :], sem.at[s])

    n_seg = n_exp + 1

    def fetch(m_ref, s):
        for e in range(n_exp):
            loff, gbase, n_chunk = m_ref[0, 0, e], m_ref[0, 0, n_seg + e], m_ref[0, 0, 2 * n_seg + e]

            def start(c, _, loff=loff, gbase=gbase):
                chunk_copy(s, pl.multiple_of(gbase + c * SUBLANES, SUBLANES),
                           pl.multiple_of(loff + c * SUBLANES, SUBLANES)).start()
                return 0
            lax.fori_loop(0, n_chunk, start, 0)

    @pl.when(i == 0)
    def _():
        fetch(meta_ref, 0)

    @pl.when(i + 1 < n_tiles)
    def _():
        fetch(meta_nxt_ref, 1 - slot)

    total = meta_ref[0, 0, 2 * n_seg]
    for e in range(1, n_exp):
        total = total + meta_ref[0, 0, 2 * n_seg + e]

    def wait(c, _):
        chunk_copy(slot, 0, 0).wait()
        return 0
    lax.fori_loop(0, total, wait, 0)

    lpos = lpos_ref[...]
    tg = tg_ref[...]
    col = lax.broadcasted_iota(jnp.int32, (tt, stage_scr.shape[1]), 1)
    sel = jnp.where(col == lpos[:, 0:1], tg[:, 0:1], 0.0)
    for k in range(1, TOP_K):
        sel = sel + jnp.where(col == lpos[:, k:k + 1], tg[:, k:k + 1], 0.0)
    y = jnp.dot(sel.astype(BF16), stage_scr[slot].astype(BF16), preferred_element_type=F32)
    o_ref[...] = _layernorm(alpha * x_ref[...] + g2_ref[0] * y, lng_ref[...], lnb_ref[...])


def _post2(ys, lpos, top_g, meta, x1, mod, ln_g, ln_b, tok_per_mod, alpha, tile_off, tt, n_exp):
    n_tok, d = x1.shape
    n_tiles = n_tok // tt
    last_tile = meta.shape[0] - 2
    stage_rows = -(-(tt * TOP_K + n_exp * SUBLANES) // LANES) * LANES
    cur = lambda i: (tile_off + i, 0, 0)
    nxt = lambda i: (jnp.minimum(tile_off + i + 1, last_tile), 0, 0)
    smem = lambda a, fn: pl.BlockSpec((1, 1, a.shape[-1]), fn, memory_space=pltpu.SMEM)
    row = lambda a: pl.BlockSpec((1, a.shape[-1]), lambda i: (0, 0))
    return pl.pallas_call(
        functools.partial(_post2_kernel, tt=tt, n_exp=n_exp, n_tiles=n_tiles, alpha=alpha),
        out_shape=jax.ShapeDtypeStruct((n_tok, d), F32),
        grid=(n_tiles,),
        in_specs=[
            pl.BlockSpec((tt, TOP_K), lambda i: (tile_off + i, 0)),
            pl.BlockSpec((tt, LANES), lambda i: (i, 0)),
            smem(meta, cur), smem(meta, nxt),
            pl.BlockSpec(memory_space=pl.ANY),
            pl.BlockSpec((tt, d), lambda i: (i, 0)),
            pl.BlockSpec((1, 1, d), lambda i: ((i * tt) // tok_per_mod, 0, 5)),
            row(ln_g), row(ln_b),
        ],
        out_specs=pl.BlockSpec((tt, d), lambda i: (i, 0)),
        scratch_shapes=[
            pltpu.VMEM((2, stage_rows, d), F32),
            pltpu.SemaphoreType.DMA((2,)),
        ],
        compiler_params=_cparams(("arbitrary",)),
        name="post2",
    )(lpos, top_g, meta, meta, ys, x1, mod, ln_g, ln_b)


def _router_weights(w_router, b_router):
    d, n_exp = w_router.shape
    hi = w_router.astype(BF16)
    lo = (w_router - hi.astype(F32)).astype(BF16)
    pad = lambda a: jnp.pad(a, ((0, 0), (0, LANES - n_exp)))
    return jnp.concatenate([pad(hi), pad(lo)], axis=1), jnp.pad(b_router, (0, LANES - n_exp)).reshape(1, LANES)


def kernel(x_prompt, x_sample, cache_k, cache_v, state_h_fwd, state_h_bwd, c, c_ctx, w_ada, b_ada, w_in, b_in,
           conv_w, conv_b, rg_wa, rg_ba, rg_wx, rg_bx, rg_lam, rpb, w_rnn_out, w_na_out, w_o, b_o,
           ln1_g, ln1_b, ln2_g, ln2_b, w_router, b_router, w_gu, b_gu, w_dn, b_dn):
    depth, d, _ = w_in.shape
    bsz, seq, _ = x_prompt.shape
    dbsz, dseq, _ = x_sample.shape
    n_heads, hd = cache_k.shape[2], cache_k.shape[4]
    d_na = n_heads * hd
    d_rnn = conv_w.shape[2]
    n_exp = w_router.shape[2]
    alpha = (2 * depth) ** 0.25
    n_p, n_s = bsz * seq, dbsz * dseq

    xp = x_prompt.reshape(n_p, d)
    xs = x_sample.reshape(n_s, d)
    n_cond = -(-(1 + dbsz) // SUBLANES) * SUBLANES
    cond = jnp.concatenate([c_ctx[None], c, jnp.zeros((n_cond - 1 - dbsz, d), F32)], axis=0)
    ks, vs, hfs, hbs = [], [], [], []
    for l in range(depth):
        mod = _ada(cond, w_ada[l], b_ada[l]).reshape(n_cond, 1, 6 * d)
        mod_p, mod_s = mod[0:1], mod[1:1 + dbsz]
        w_in_l = w_in[l].astype(BF16)
        w_gates = jnp.concatenate([rg_wa[l], rg_wx[l]], axis=-1).astype(BF16)
        w_rnn_l, w_na_l, w_o_l = w_rnn_out[l].astype(BF16), w_na_out[l].astype(BF16), w_o[l].astype(BF16)
        w_rt, b_rt = _router_weights(w_router[l], b_router[l])
        row = lambda a: a[l].reshape(1, -1)

        def mixer(x, mod_g, n_b, t, h0, tok_per_mod, is_ctx):
            xr, qkv, g = _inproj(x, mod_g, w_in_l, b_in[l], d_rnn, d_na, tok_per_mod, F32 if is_ctx else BF16)
            yf, yb, h_last = _rglru(xr.reshape(n_b, t, d_rnn), conv_w[l], conv_b[l], w_gates,
                                    rg_ba[l], rg_bx[l], rg_lam[l], h0, is_ctx)
            qkv3 = qkv.reshape(n_b, t, 3 * d_na)
            if is_ctx:
                ya = _cattn(qkv3, n_heads, hd)
            else:
                ya = _nattn(qkv3, cache_k[:, l], cache_v[:, l], rpb[l], n_heads, hd)
            merged = _merge(yf.reshape(n_b * t, d_rnn), yb.reshape(n_b * t, d_rnn), ya.reshape(n_b * t, d_na),
                            g, w_rnn_l, w_na_l)
            x1, u2, ti, tg = _post1(merged, x, mod_g, w_o_l, row(b_o), row(ln1_g), row(ln1_b), w_rt, b_rt,
                                    tok_per_mod, alpha, n_exp)
            return x1, u2, ti, tg, qkv3, h_last

        x1p, u2p, tip, tgp, qkv_p, hl_p = mixer(xp, mod_p, bsz, seq, jnp.zeros((bsz, 2, d_rnn), F32), n_p, True)
        h0_s = jnp.stack([state_h_fwd[:, l], state_h_bwd[:, l]], axis=1)
        x1s, u2s, tis, tgs, _, _ = mixer(xs, mod_s, dbsz, dseq, h0_s, dseq, False)

        n_all = n_p + n_s
        tt = _pick(np.gcd(n_p, dseq), MOE_TT)
        tm = MOE_TM
        top_i = jnp.concatenate([tip[:, :TOP_K], tis[:, :TOP_K]], axis=0)
        lpos, meta, blk_e, blk_rows, n_blocks = _route(top_i, n_exp, tt, tm)
        xsort = _dispatch(u2p, u2s, lpos, meta, n_blocks * tm, tt, tm, n_exp)
        ys = _moe(xsort, blk_e, blk_rows, w_gu[l], b_gu[l], w_dn[l], b_dn[l], tm)
        lpos_tok = lpos.reshape(n_all, TOP_K)
        xp = _post2(ys, lpos_tok, tgp, meta, x1p, mod_p, row(ln2_g), row(ln2_b), n_p, alpha, 0, tt, n_exp)
        xs = _post2(ys, lpos_tok, tgs, meta, x1s, mod_s, row(ln2_g), row(ln2_b), dseq, alpha, n_p // tt, tt, n_exp)

        heads = lambda a: a.reshape(bsz, seq, n_heads, hd).transpose(0, 2, 1, 3)
        ks.append(heads(qkv_p[:, :, d_na:2 * d_na]))
        vs.append(heads(qkv_p[:, :, 2 * d_na:]))
        hfs.append(hl_p[:, 0])
        hbs.append(hl_p[:, 1])

    return (xp.reshape(bsz, seq, d), xs.reshape(dbsz, dseq, d),
            jnp.stack(ks, axis=1), jnp.stack(vs, axis=1), jnp.stack(hfs, axis=1), jnp.stack(hbs, axis=1))
```

```python
import functools

import numpy as np
import jax
import jax.numpy as jnp
from jax import lax
from jax.experimental import pallas as pl
from jax.experimental.pallas import tpu as pltpu

F32 = jnp.float32
BF16 = jnp.bfloat16

GRID_W = 64
WIN_ROWS = 8
WIN_COLS = 16
CONV_W = 4
RG_C = 8.0
TOP_K = 4
SWIGLU_LIMIT = 7.0
SWIGLU_ALPHA = 1.702
LN_EPS = 1e-5
NEG_INF = -1e30

VMEM_LIMIT_BYTES = 56 * 1024 * 1024
SUBLANES = 8
LANES = 128

NATTN_ROWS = 4
NATTN_SUB = 8
MOE_TT = 256
MOE_TM = 1024
MOE_TF = 256
MOE_ROW_CHUNKS = 2
ROW_CHUNKS = 2
DMA_PRIORITIES = 2


def _cparams(sem):
    return pltpu.CompilerParams(dimension_semantics=sem, vmem_limit_bytes=VMEM_LIMIT_BYTES)


def _pick(n, pref):
    t = min(n, pref)
    while n % t:
        t //= 2
    return t


def _ada_kernel(c_ref, w_ref, b_ref, o_ref):
    c = c_ref[...]
    s = c * jax.nn.sigmoid(c)
    o_ref[...] = jnp.dot(s.astype(BF16), w_ref[...].astype(BF16), preferred_element_type=F32) + b_ref[...]


def _ada(cond, w_ada, b_ada):
    nb, d = cond.shape
    n = w_ada.shape[1]
    tn = _pick(n, 1024)
    return pl.pallas_call(
        _ada_kernel,
        out_shape=jax.ShapeDtypeStruct((nb, n), F32),
        grid=(n // tn,),
        in_specs=[
            pl.BlockSpec((nb, d), lambda j: (0, 0)),
            pl.BlockSpec((d, tn), lambda j: (0, j)),
            pl.BlockSpec((1, tn), lambda j: (0, j)),
        ],
        out_specs=pl.BlockSpec((nb, tn), lambda j: (0, j)),
        compiler_params=_cparams(("arbitrary",)),
        name="ada",
    )(cond, w_ada, b_ada.reshape(1, n))


def _inproj_kernel(x_ref, sh_ref, sc_ref, w_ref, b_ref, xr_ref, qkv_ref, g_ref, u_scr, *, n_xr, n_qkv):
    n = pl.program_id(1)

    @pl.when(n == 0)
    def _():
        u = x_ref[...] * (1.0 + sc_ref[0]) + sh_ref[0]
        u_scr[...] = u.astype(BF16)

    def project(o_ref):
        rows = u_scr.shape[0] // ROW_CHUNKS
        for c in range(ROW_CHUNKS):
            sl = pl.ds(c * rows, rows)
            acc = jnp.dot(u_scr[sl, :], w_ref[...], preferred_element_type=F32) + b_ref[...]
            o_ref[sl, :] = acc.astype(o_ref.dtype)

    @pl.when(n < n_xr)
    def _():
        project(xr_ref)

    @pl.when((n >= n_xr) & (n < n_xr + n_qkv))
    def _():
        project(qkv_ref)

    @pl.when(n >= n_xr + n_qkv)
    def _():
        project(g_ref)


def _inproj(x, mod, w_in, b_in, d_rnn, d_na, tok_per_mod, qkv_dtype):
    n_tok, d = x.shape
    d_in = w_in.shape[1]
    tn = _pick(np.gcd(np.gcd(d_rnn, d_na), d), 1024)
    tm = _pick(tok_per_mod, 1024 if jnp.dtype(qkv_dtype).itemsize == 2 else 512)
    n_xr, n_qkv, n_g = d_rnn // tn, 3 * d_na // tn, 2 * d // tn
    assert (n_xr + n_qkv + n_g) * tn == d_in

    def mod_map(col):
        return lambda i, j: ((i * tm) // tok_per_mod, 0, col)

    return pl.pallas_call(
        functools.partial(_inproj_kernel, n_xr=n_xr, n_qkv=n_qkv),
        out_shape=(
            jax.ShapeDtypeStruct((n_tok, d_rnn), F32),
            jax.ShapeDtypeStruct((n_tok, 3 * d_na), qkv_dtype),
            jax.ShapeDtypeStruct((n_tok, 2 * d), BF16),
        ),
        grid=(n_tok // tm, d_in // tn),
        in_specs=[
            pl.BlockSpec((tm, d), lambda i, j: (i, 0)),
            pl.BlockSpec((1, 1, d), mod_map(0)),
            pl.BlockSpec((1, 1, d), mod_map(1)),
            pl.BlockSpec((d, tn), lambda i, j: (0, j)),
            pl.BlockSpec((1, tn), lambda i, j: (0, j)),
        ],
        out_specs=(
            pl.BlockSpec((tm, tn), lambda i, j: (i, jnp.minimum(j, n_xr - 1))),
            pl.BlockSpec((tm, tn), lambda i, j: (i, jnp.clip(j - n_xr, 0, n_qkv - 1))),
            pl.BlockSpec((tm, tn), lambda i, j: (i, jnp.clip(j - n_xr - n_qkv, 0, n_g - 1))),
        ),
        scratch_shapes=[pltpu.VMEM((tm, d), BF16)],
        compiler_params=_cparams(("arbitrary", "arbitrary")),
        name="inproj",
    )(x, mod, mod, w_in, b_in.reshape(1, d_in))


def _rglru_kernel(xf_ref, xfp_ref, xfn_ref, xb_ref, xbp_ref, xbn_ref,
                  cw_ref, cb_ref, w_ref, ba_ref, bx_ref, lam_ref, h0_ref,
                  yf_ref, yb_ref, hl_ref,
                  xext_scr, a_scr, b_scr, carry_scr, *, t_tile, n_t, seq_len, reset):
    j = pl.program_id(2)
    c = xf_ref.shape[-1]
    n_grp = t_tile // SUBLANES

    @pl.when(j == 0)
    def _():
        carry_scr[...] = h0_ref[0]

    row = lax.broadcasted_iota(jnp.int32, (t_tile, 1), 0)
    sub3 = lax.broadcasted_iota(jnp.int32, (1, SUBLANES, 1), 1)

    def direction(d, x_ref, xp_ref, xn_ref, y_ref):
        jt = j if d == 0 else n_t - 1 - j
        xext_scr[0:SUBLANES, :] = jnp.where(jt > 0, xp_ref[0], 0.0)
        xext_scr[SUBLANES:SUBLANES + t_tile, :] = x_ref[0]
        xext_scr[SUBLANES + t_tile:2 * SUBLANES + t_tile, :] = jnp.where(jt < n_t - 1, xn_ref[0], 0.0)
        left = CONV_W // 2
        xc = cb_ref[...] + xext_scr[pl.ds(SUBLANES - left, t_tile), :] * cw_ref[0:1, :]
        for k in range(1, CONV_W):
            xc = xc + xext_scr[pl.ds(SUBLANES - left + k, t_tile), :] * cw_ref[k:k + 1, :]
        pre = jnp.dot(xc.astype(BF16), w_ref[d, 0], preferred_element_type=F32)
        r_gate = jax.nn.sigmoid(pre[:, :c] + ba_ref[d:d + 1, :])
        i_gate = jax.nn.sigmoid(pre[:, c:] + bx_ref[d:d + 1, :])
        lam = lam_ref[d:d + 1, :]
        softplus_neg_lam = jnp.maximum(-lam, 0.0) + jnp.log1p(jnp.exp(-jnp.abs(lam)))
        log_a = -RG_C * r_gate * softplus_neg_lam
        a = jnp.exp(log_a)
        om = 1.0 - a * a
        mult = jnp.where(om > 0.0, om * lax.rsqrt(om), 0.0)
        if reset:
            first = 0 if d == 0 else seq_len - 1
            mult = jnp.where(jt * t_tile + row == first, 1.0, mult)
        b = mult * i_gate * xc
        a = a.reshape(n_grp, SUBLANES, c)
        b = b.reshape(n_grp, SUBLANES, c)
        for s in (1, 2, 4):
            if d == 0:
                a_sh = pltpu.roll(a, s, axis=1)
                b_sh = pltpu.roll(b, s, axis=1)
                m = sub3 >= s
            else:
                a_sh = pltpu.roll(a, SUBLANES - s, axis=1)
                b_sh = pltpu.roll(b, SUBLANES - s, axis=1)
                m = sub3 < SUBLANES - s
            b = jnp.where(m, b + a * b_sh, b)
            a = jnp.where(m, a * a_sh, a)
        a_scr[...] = a.reshape(t_tile, c)
        b_scr[...] = b.reshape(t_tile, c)

        def body(g, h):
            gg = g if d == 0 else n_grp - 1 - g
            off = pl.multiple_of(gg * SUBLANES, SUBLANES)
            h_rows = b_scr[pl.ds(off, SUBLANES), :] + a_scr[pl.ds(off, SUBLANES), :] * h
            y_ref[0, pl.ds(off, SUBLANES), :] = h_rows.astype(y_ref.dtype)
            return h_rows[SUBLANES - 1:SUBLANES, :] if d == 0 else h_rows[0:1, :]

        h = lax.fori_loop(0, n_grp, body, carry_scr[d:d + 1, :], unroll=4)
        carry_scr[d:d + 1, :] = h
        hl_ref[0, d:d + 1, :] = h

    direction(0, xf_ref, xfp_ref, xfn_ref, yf_ref)
    direction(1, xb_ref, xbp_ref, xbn_ref, yb_ref)


def _rglru(xr, conv_w, conv_b, w_gates, rg_ba, rg_bx, rg_lam, h0, reset):
    bsz, t, d_rnn = xr.shape
    n_blk, c = w_gates.shape[1], w_gates.shape[2]
    t_tile = _pick(t, 1024)
    n_t = t // t_tile
    tb = t_tile // SUBLANES
    last_blk = t // SUBLANES - 1

    def main(rev):
        return lambda b, n, j: (b, (n_t - 1 - j) if rev else j, n)

    def prev(rev):
        return lambda b, n, j: (b, jnp.maximum(((n_t - 1 - j) if rev else j) * tb - 1, 0), n)

    def nxt(rev):
        return lambda b, n, j: (b, jnp.minimum((((n_t - 1 - j) if rev else j) + 1) * tb, last_blk), n)

    vec = lambda rows: pl.BlockSpec((rows, c), lambda b, n, j: (0, n))
    return pl.pallas_call(
        functools.partial(_rglru_kernel, t_tile=t_tile, n_t=n_t, seq_len=t, reset=reset),
        out_shape=(
            jax.ShapeDtypeStruct((bsz, t, d_rnn), BF16),
            jax.ShapeDtypeStruct((bsz, t, d_rnn), BF16),
            jax.ShapeDtypeStruct((bsz, 2, d_rnn), F32),
        ),
        grid=(bsz, n_blk, n_t),
        in_specs=[
            pl.BlockSpec((1, t_tile, c), main(False)),
            pl.BlockSpec((1, SUBLANES, c), prev(False)),
            pl.BlockSpec((1, SUBLANES, c), nxt(False)),
            pl.BlockSpec((1, t_tile, c), main(True)),
            pl.BlockSpec((1, SUBLANES, c), prev(True)),
            pl.BlockSpec((1, SUBLANES, c), nxt(True)),
            vec(CONV_W),
            vec(1),
            pl.BlockSpec((2, 1, c, 2 * c), lambda b, n, j: (0, n, 0, 0)),
            vec(2), vec(2), vec(2),
            pl.BlockSpec((1, 2, c), lambda b, n, j: (b, 0, n)),
        ],
        out_specs=(
            pl.BlockSpec((1, t_tile, c), main(False)),
            pl.BlockSpec((1, t_tile, c), main(True)),
            pl.BlockSpec((1, 2, c), lambda b, n, j: (b, 0, n)),
        ),
        scratch_shapes=[
            pltpu.VMEM((t_tile + 2 * SUBLANES, c), F32),
            pltpu.VMEM((t_tile, c), F32),
            pltpu.VMEM((t_tile, c), F32),
            pltpu.VMEM((2, c), F32),
        ],
        compiler_params=_cparams(("arbitrary", "arbitrary", "arbitrary")),
        name="rglru",
    )(xr, xr, xr, xr, xr, xr, conv_w, conv_b.reshape(1, d_rnn), w_gates, rg_ba, rg_bx, rg_lam, h0)


def _qk(q, k):
    return lax.dot_general(q, k, (((1,), (1,)), ((), ())), preferred_element_type=F32)


def _cattn_kernel(q_ref, k_ref, v_ref, o_ref, *, n_heads, hd, scale):
    for h in range(n_heads):
        sl = slice(h * hd, (h + 1) * hd)
        q = q_ref[0, :, sl].astype(BF16)
        k = k_ref[0, :, sl].astype(BF16)
        v = v_ref[0, :, sl].astype(BF16)
        s = _qk(q, k) * scale
        p = jnp.exp(s - jnp.max(s, axis=-1, keepdims=True))
        l = jnp.sum(p, axis=-1, keepdims=True)
        o = jnp.dot(p.astype(BF16), v, preferred_element_type=F32) / l
        o_ref[0, :, sl] = o.astype(o_ref.dtype)


def _cattn(qkv, n_heads, hd):
    bsz, s, _ = qkv.shape
    d_na = n_heads * hd
    spec = lambda col: pl.BlockSpec((1, s, d_na), lambda b: (b, 0, col))
    return pl.pallas_call(
        functools.partial(_cattn_kernel, n_heads=n_heads, hd=hd, scale=hd ** -0.5),
        out_shape=jax.ShapeDtypeStruct((bsz, s, d_na), BF16),
        grid=(bsz,),
        in_specs=[spec(0), spec(1), spec(2)],
        out_specs=pl.BlockSpec((1, s, d_na), lambda b: (b, 0, 0)),
        compiler_params=_cparams(("arbitrary",)),
        name="cattn",
    )(qkv, qkv, qkv)


def _nattn_window_start(rb, rows, r_blk, w_blk):
    return jnp.clip(rb * r_blk - WIN_ROWS // 2, 0, rows - w_blk)


def _nattn_kernel(q_ref, k_ref, v_ref, ck_ref, cv_ref, bias_ref, o_ref, *, hd, scale, rows, r_blk, w_blk, n_sub):
    step = pl.program_id(1)
    n_rb = rows // r_blk
    tq = r_blk * GRID_W
    for s in range(n_sub):
        rb = step * n_sub + s
        pat = jnp.where(rb == 0, 0, jnp.where(rb == n_rb - 1, 2, 1))
        w0 = _nattn_window_start(rb, rows, r_blk, w_blk)
        start = pl.multiple_of(w0 * GRID_W, GRID_W)
        k_win = k_ref[0, pl.ds(start, w_blk * GRID_W), :]
        v_win = v_ref[0, pl.ds(start, w_blk * GRID_W), :]
        q = q_ref[0, s * tq:(s + 1) * tq, :]
        outs = []
        for h in range(q.shape[-1] // hd):
            sl = slice(h * hd, (h + 1) * hd)
            qh = q[:, sl]
            s_lat = _qk(qh, k_win[:, sl]) * scale + bias_ref[h, pat]
            s_ctx = _qk(qh, ck_ref[0, h].astype(BF16)) * scale
            m = jnp.maximum(jnp.max(s_lat, axis=-1, keepdims=True), jnp.max(s_ctx, axis=-1, keepdims=True))
            p_lat = jnp.exp(s_lat - m)
            p_ctx = jnp.exp(s_ctx - m)
            l = jnp.sum(p_lat, axis=-1, keepdims=True) + jnp.sum(p_ctx, axis=-1, keepdims=True)
            o = (jnp.dot(p_lat.astype(BF16), v_win[:, sl], preferred_element_type=F32)
                 + jnp.dot(p_ctx.astype(BF16), cv_ref[0, h].astype(BF16), preferred_element_type=F32))
            outs.append(o / l)
        o_ref[0, s * tq:(s + 1) * tq, :] = jnp.concatenate(outs, axis=-1).astype(o_ref.dtype)


def _nattn_bias(rpb, rows, r_blk, w_blk):
    n_rb = rows // r_blk
    wr = min(WIN_ROWS, rows)
    n_heads = rpb.shape[0]
    col = np.arange(GRID_W)
    cstart = np.clip(col - WIN_COLS // 2, 0, GRID_W - WIN_COLS)
    col_ok = (col[None, :] >= cstart[:, None]) & (col[None, :] < cstart[:, None] + WIN_COLS)
    rp = jnp.pad(rpb.astype(F32), ((0, 0), (0, 0), (GRID_W, GRID_W)))
    shifted = [rp[:, :, GRID_W + WIN_COLS - 1 - qc:2 * GRID_W + WIN_COLS - 1 - qc] for qc in range(GRID_W)]
    tiles = jnp.where(col_ok[None, None], jnp.stack(shifted, axis=2), NEG_INF)
    masked = jnp.full((n_heads, GRID_W, GRID_W), NEG_INF, F32)
    pats = []
    for rb in (0, min(1, n_rb - 1), n_rb - 1):
        r0 = rb * r_blk
        w0 = int(np.clip(r0 - WIN_ROWS // 2, 0, rows - w_blk))
        strips = []
        for i in range(r_blk):
            qrow = r0 + i
            rstart = int(np.clip(qrow - wr // 2, 0, rows - wr))
            strip = []
            for jj in range(w_blk):
                krow = w0 + jj
                ok = rstart <= krow < rstart + wr
                strip.append(tiles[:, krow - qrow + WIN_ROWS - 1] if ok else masked)
            strips.append(jnp.concatenate(strip, axis=-1))
        pats.append(jnp.concatenate(strips, axis=-2))
    return jnp.stack(pats, axis=1)


def _nattn(qkv, ck, cv, rpb, n_heads, hd):
    bsz, t, _ = qkv.shape
    d_na = n_heads * hd
    rows = t // GRID_W
    r_blk = min(NATTN_ROWS, rows)
    w_blk = min(r_blk + WIN_ROWS - 1, rows)
    n_rb = rows // r_blk
    for rb in range(1, n_rb - 1):
        assert 0 <= rb * r_blk - WIN_ROWS // 2 <= rows - w_blk
    hp = LANES // hd
    n_pair = d_na // LANES
    past = ck.shape[2]
    bias = _nattn_bias(rpb, rows, r_blk, w_blk)
    n_sub = _pick(n_rb, NATTN_SUB)
    tq, tk = n_sub * r_blk * GRID_W, w_blk * GRID_W

    return pl.pallas_call(
        functools.partial(_nattn_kernel, hd=hd, scale=hd ** -0.5, rows=rows, r_blk=r_blk, w_blk=w_blk,
                          n_sub=n_sub),
        out_shape=jax.ShapeDtypeStruct((bsz, t, d_na), BF16),
        grid=(n_pair, n_rb // n_sub, bsz),
        in_specs=[
            pl.BlockSpec((1, tq, LANES), lambda p, rb, b: (b, rb, p)),
            pl.BlockSpec((1, t, LANES), lambda p, rb, b: (b, 0, n_pair + p)),
            pl.BlockSpec((1, t, LANES), lambda p, rb, b: (b, 0, 2 * n_pair + p)),
            pl.BlockSpec((1, hp, past, hd), lambda p, rb, b: (b, p, 0, 0)),
            pl.BlockSpec((1, hp, past, hd), lambda p, rb, b: (b, p, 0, 0)),
            pl.BlockSpec((hp, 3, r_blk * GRID_W, tk), lambda p, rb, b: (p, 0, 0, 0)),
        ],
        out_specs=pl.BlockSpec((1, tq, LANES), lambda p, rb, b: (b, rb, p)),
        compiler_params=_cparams(("arbitrary", "arbitrary", "arbitrary")),
        name="nattn",
    )(qkv, qkv, qkv, ck, cv, bias)


def _merge_kernel(yf_ref, yb_ref, ya_ref, ga_ref, gb_ref, wr_ref, wa_ref, o_ref):
    rows = o_ref.shape[0] // ROW_CHUNKS
    for c in range(ROW_CHUNKS):
        sl = pl.ds(c * rows, rows)
        y_rnn = (yf_ref[sl, :].astype(F32) + yb_ref[sl, :].astype(F32)).astype(BF16)
        t_rnn = jnp.dot(y_rnn, wr_ref[...], preferred_element_type=F32)
        t_na = jnp.dot(ya_ref[sl, :], wa_ref[...], preferred_element_type=F32)
        merged = (jax.nn.sigmoid(ga_ref[sl, :].astype(F32)) * t_rnn
                  + jax.nn.sigmoid(gb_ref[sl, :].astype(F32)) * t_na)
        o_ref[sl, :] = merged.astype(o_ref.dtype)


def _merge(yf, yb, ya, g, w_rnn_out, w_na_out):
    n_tok, d_rnn = yf.shape
    d_na = ya.shape[1]
    d = w_rnn_out.shape[1]
    tm = _pick(n_tok, 512)
    tn = _pick(d, 1024)
    n_n = d // tn
    return pl.pallas_call(
        _merge_kernel,
        out_shape=jax.ShapeDtypeStruct((n_tok, d), BF16),
        grid=(n_n, n_tok // tm),
        in_specs=[
            pl.BlockSpec((tm, d_rnn), lambda j, i: (i, 0)),
            pl.BlockSpec((tm, d_rnn), lambda j, i: (i, 0)),
            pl.BlockSpec((tm, d_na), lambda j, i: (i, 0)),
            pl.BlockSpec((tm, tn), lambda j, i: (i, j)),
            pl.BlockSpec((tm, tn), lambda j, i: (i, n_n + j)),
            pl.BlockSpec((d_rnn, tn), lambda j, i: (0, j)),
            pl.BlockSpec((d_na, tn), lambda j, i: (0, j)),
        ],
        out_specs=pl.BlockSpec((tm, tn), lambda j, i: (i, j)),
        compiler_params=_cparams(("arbitrary", "arbitrary")),
        name="merge",
    )(yf, yb, ya, g, g, w_rnn_out, w_na_out)


def _layernorm(z, g, b):
    mu = jnp.mean(z, axis=-1, keepdims=True)
    zc = z - mu
    var = jnp.mean(zc * zc, axis=-1, keepdims=True)
    return zc * lax.rsqrt(var + LN_EPS) * g + b


def _pack_bf16_pairs(x):
    half = x.shape[-1] // 2
    lo = lax.bitcast_convert_type(x[:, :half].astype(BF16).astype(F32), jnp.uint32)
    hi = lax.bitcast_convert_type(x[:, half:].astype(BF16).astype(F32), jnp.uint32)
    return (lo >> 16) | (hi & jnp.uint32(0xFFFF0000))


def _unpack_bf16_pairs(w):
    lo = lax.bitcast_convert_type(w << 16, F32).astype(BF16)
    hi = lax.bitcast_convert_type(w & jnp.uint32(0xFFFF0000), F32).astype(BF16)
    return lo, hi


def _post1_kernel(m_ref, x_ref, g1_ref, sh2_ref, sc2_ref, wo_ref, bo_ref, lng_ref, lnb_ref, wrt_ref, brt_ref,
                  x1_ref, u2_ref, ti_ref, tg_ref, *, alpha, n_experts):
    rows = x_ref.shape[0] // ROW_CHUNKS
    for c in range(ROW_CHUNKS):
        _post1_rows(pl.ds(c * rows, rows), m_ref, x_ref, g1_ref, sh2_ref, sc2_ref, wo_ref, bo_ref, lng_ref, lnb_ref,
                    wrt_ref, brt_ref, x1_ref, u2_ref, ti_ref, tg_ref, alpha, n_experts)


def _post1_rows(sl, m_ref, x_ref, g1_ref, sh2_ref, sc2_ref, wo_ref, bo_ref, lng_ref, lnb_ref, wrt_ref, brt_ref,
                x1_ref, u2_ref, ti_ref, tg_ref, alpha, n_experts):
    mix = jnp.dot(m_ref[sl, :], wo_ref[...], preferred_element_type=F32) + bo_ref[...]
    x1 = _layernorm(alpha * x_ref[sl, :] + g1_ref[0] * mix, lng_ref[...], lnb_ref[...])
    x1_ref[sl, :] = x1
    u2 = x1 * (1.0 + sc2_ref[0]) + sh2_ref[0]
    u2_ref[sl, :] = _pack_bf16_pairs(u2)
    hi = u2.astype(BF16)
    lo = (u2 - hi.astype(F32)).astype(BF16)
    ph = jnp.dot(hi, wrt_ref[...], preferred_element_type=F32)
    plo = jnp.dot(lo, wrt_ref[...], preferred_element_type=F32)
    logits = (ph[:, :LANES] + ph[:, LANES:]) + (plo[:, :LANES] + plo[:, LANES:]) + brt_ref[...]
    lane = lax.broadcasted_iota(jnp.int32, logits.shape, 1)
    lane_f = lane.astype(F32)
    logits = jnp.where(lane < n_experts, logits, NEG_INF)
    top_i = jnp.zeros(logits.shape, jnp.int32)
    top_e = jnp.zeros(logits.shape, F32)
    v0 = None
    denom = None
    for k in range(TOP_K):
        v = jnp.max(logits, axis=-1, keepdims=True)
        idx = jnp.min(jnp.where(logits == v, lane_f, float(LANES)), axis=-1, keepdims=True).astype(jnp.int32)
        if k == 0:
            v0 = v
        e = jnp.exp(v - v0)
        denom = e if k == 0 else denom + e
        top_i = jnp.where(lane == k, idx, top_i)
        top_e = jnp.where(lane == k, e, top_e)
        logits = jnp.where(lane == idx, NEG_INF, logits)
    ti_ref[sl, :] = top_i
    tg_ref[sl, :] = top_e / denom


def _post1(merged, x, mod, w_o, b_o, ln_g, ln_b, w_rt, b_rt, tok_per_mod, alpha, n_experts):
    n_tok, d = x.shape
    tm = _pick(tok_per_mod, 512)

    def mod_map(col):
        return lambda i: ((i * tm) // tok_per_mod, 0, col)

    row = lambda a: pl.BlockSpec((1, a.shape[-1]), lambda i: (0, 0))
    tile = pl.BlockSpec((tm, d), lambda i: (i, 0))
    small = pl.BlockSpec((tm, LANES), lambda i: (i, 0))
    return pl.pallas_call(
        functools.partial(_post1_kernel, alpha=alpha, n_experts=n_experts),
        out_shape=(
            jax.ShapeDtypeStruct((n_tok, d), F32),
            jax.ShapeDtypeStruct((n_tok, d // 2), jnp.uint32),
            jax.ShapeDtypeStruct((n_tok, LANES), jnp.int32),
            jax.ShapeDtypeStruct((n_tok, LANES), F32),
        ),
        grid=(n_tok // tm,),
        in_specs=[
            tile, tile,
            pl.BlockSpec((1, 1, d), mod_map(2)),
            pl.BlockSpec((1, 1, d), mod_map(3)),
            pl.BlockSpec((1, 1, d), mod_map(4)),
            pl.BlockSpec((d, d), lambda i: (0, 0)),
            row(b_o), row(ln_g), row(ln_b),
            pl.BlockSpec((d, 2 * LANES), lambda i: (0, 0)),
            row(b_rt),
        ],
        out_specs=(tile, pl.BlockSpec((tm, d // 2), lambda i: (i, 0)), small, small),
        compiler_params=_cparams(("arbitrary",)),
        name="post1",
    )(merged, x, mod, mod, mod, w_o, b_o, ln_g, ln_b, w_rt, b_rt)


def _dispatch_kernel(lpos_ref, meta_ref, xa_ref, xb_ref, xs_hbm, stage_scr, sem, bsem, nprev_scr,
                     *, tt, tm, n_exp, n_tiles_a, n_tiles):
    i = pl.program_id(0)
    slot = i % 2
    n_seg = n_exp + 1
    is_fill = i == n_tiles

    @pl.when(i == 0)
    def _():
        stage_scr[...] = jnp.zeros(stage_scr.shape, stage_scr.dtype)

    @pl.when(is_fill)
    def _():
        stage_scr[slot] = jnp.zeros(stage_scr.shape[1:], stage_scr.dtype)

    def place_from(x_ref):
        def place(t, _):
            row = x_ref[pl.ds(t, 1), :]
            for k in range(TOP_K):
                stage_scr[slot, pl.ds(lpos_ref[0, 0, t * TOP_K + k], 1), :] = row
            return 0
        lax.fori_loop(0, tt, place, 0, unroll=2)

    @pl.when(i < n_tiles_a)
    def _():
        place_from(xa_ref)

    @pl.when((i >= n_tiles_a) & jnp.logical_not(is_fill))
    def _():
        place_from(xb_ref)

    def chunk_copy(s, src_row, dst_row):
        return pltpu.make_async_copy(stage_scr.at[s, pl.ds(src_row, SUBLANES), :],
                                     xs_hbm.at[pl.ds(dst_row, SUBLANES), :], sem.at[s])

    def block_copy(dst_row):
        return pltpu.make_async_copy(stage_scr.at[slot, pl.ds(0, tm), :], xs_hbm.at[pl.ds(dst_row, tm), :], bsem)

    src_step = jnp.where(is_fill, 0, SUBLANES)
    total = 0
    for e in range(n_exp):
        loff, gbase, n_chunk = meta_ref[0, 0, e], meta_ref[0, 0, n_seg + e], meta_ref[0, 0, 2 * n_seg + e]

        def start(c, _, loff=loff, gbase=gbase):
            chunk_copy(slot, pl.multiple_of(loff + c * src_step, SUBLANES),
                       pl.multiple_of(gbase + c * SUBLANES, SUBLANES)).start(priority=e % DMA_PRIORITIES)
            return 0
        lax.fori_loop(0, n_chunk, start, 0)
        total = total + n_chunk

    def drain(s, n):
        def wait(c, _):
            chunk_copy(s, 0, 0).wait()
            return 0
        lax.fori_loop(0, n, wait, 0)

    @pl.when(i > 0)
    def _():
        drain(1 - slot, nprev_scr[0])
    nprev_scr[0] = total

    @pl.when(is_fill)
    def _():
        first_row, n_blk = meta_ref[0, 0, n_seg + n_exp], meta_ref[0, 0, 2 * n_seg + n_exp]

        def start(b, _):
            block_copy(pl.multiple_of(first_row + b * tm, tm)).start()
            return 0
        lax.fori_loop(0, n_blk, start, 0)

        def wait(b, _):
            block_copy(0).wait()
            return 0
        lax.fori_loop(0, n_blk, wait, 0)
        drain(slot, total)


def _dispatch(u_a, u_b, lpos, meta, n_slots, tt, tm, n_exp):
    dw = u_a.shape[1]
    n_tiles_a, n_tiles_b = u_a.shape[0] // tt, u_b.shape[0] // tt
    n_tiles = n_tiles_a + n_tiles_b
    stage_rows = max(tt * TOP_K + n_exp * SUBLANES, tm)
    return pl.pallas_call(
        functools.partial(_dispatch_kernel, tt=tt, tm=tm, n_exp=n_exp, n_tiles_a=n_tiles_a, n_tiles=n_tiles),
        out_shape=jax.ShapeDtypeStruct((n_slots, dw), u_a.dtype),
        grid=(n_tiles + 1,),
        in_specs=[
            pl.BlockSpec((1, 1, lpos.shape[-1]), lambda i: (jnp.minimum(i, n_tiles - 1), 0, 0),
                         memory_space=pltpu.SMEM),
            pl.BlockSpec((1, 1, meta.shape[-1]), lambda i: (i, 0, 0), memory_space=pltpu.SMEM),
            pl.BlockSpec((tt, dw), lambda i: (jnp.minimum(i, n_tiles_a - 1), 0)),
            pl.BlockSpec((tt, dw), lambda i: (jnp.clip(i - n_tiles_a, 0, n_tiles_b - 1), 0)),
        ],
        out_specs=pl.BlockSpec(memory_space=pl.ANY),
        scratch_shapes=[
            pltpu.VMEM((2, stage_rows, dw), u_a.dtype),
            pltpu.SemaphoreType.DMA((2,)),
            pltpu.SemaphoreType.DMA,
            pltpu.SMEM((1,), jnp.int32),
        ],
        compiler_params=_cparams(("arbitrary",)),
        name="dispatch",
    )(lpos, meta, u_a, u_b)


def _moe_kernel(be_ref, br_ref, x_ref, wg_ref, wu_ref, wd_ref, bg_ref, bu_ref, bd_ref, o_ref, xb_scr, *, n_f):
    i = pl.program_id(0)
    f = pl.program_id(1)
    n_rows = br_ref[i]
    half = x_ref.shape[-1]

    @pl.when((n_rows > 0) & (f == 0))
    def _():
        lo, hi = _unpack_bf16_pairs(x_ref[...])
        xb_scr[:, :half] = lo
        xb_scr[:, half:] = hi
        o_ref[...] = jnp.broadcast_to(bd_ref[0], o_ref.shape)

    @pl.when(n_rows > 0)
    def _():
        wg = wg_ref[0].astype(BF16)
        wu = wu_ref[0].astype(BF16)
        wd = wd_ref[0].astype(BF16)
        rows = x_ref.shape[0] // MOE_ROW_CHUNKS
        for c in range(MOE_ROW_CHUNKS):
            sl = pl.ds(c * rows, rows)
            xb = xb_scr[sl, :]
            gate = jnp.dot(xb, wg, preferred_element_type=F32) + bg_ref[0]
            up = jnp.dot(xb, wu, preferred_element_type=F32) + bu_ref[0]
            gate = jnp.minimum(gate, SWIGLU_LIMIT)
            up = jnp.clip(up, -SWIGLU_LIMIT, SWIGLU_LIMIT)
            glu = gate * jax.nn.sigmoid(SWIGLU_ALPHA * gate)
            act = ((up + 1.0) * glu).astype(BF16)
            o_ref[sl, :] += jnp.dot(act, wd, preferred_element_type=F32)

    @pl.when((n_rows == 0) & (f == 0))
    def _():
        o_ref[...] = jnp.zeros(o_ref.shape, o_ref.dtype)


def _moe(xs, blk_e, blk_rows, w_gu, b_gu, w_dn, b_dn, tm):
    n_slots, dw = xs.shape
    n_blocks = n_slots // tm
    n_exp, d, two_ff = w_gu.shape
    d_ff = two_ff // 2
    tf = _pick(d_ff, MOE_TF)
    n_f = d_ff // tf

    def fe(f, br, i):
        return jnp.where(br[i] > 0, f, n_f - 1)

    grid_spec = pltpu.PrefetchScalarGridSpec(
        num_scalar_prefetch=2,
        grid=(n_blocks, n_f),
        in_specs=[
            pl.BlockSpec((tm, dw), lambda i, f, be, br: (jnp.where(br[i] > 0, i, 0), 0)),
            pl.BlockSpec((1, d, tf), lambda i, f, be, br: (be[i], 0, fe(f, br, i))),
            pl.BlockSpec((1, d, tf), lambda i, f, be, br: (be[i], 0, n_f + fe(f, br, i))),
            pl.BlockSpec((1, tf, d), lambda i, f, be, br: (be[i], fe(f, br, i), 0)),
            pl.BlockSpec((1, 1, tf), lambda i, f, be, br: (be[i], 0, fe(f, br, i))),
            pl.BlockSpec((1, 1, tf), lambda i, f, be, br: (be[i], 0, n_f + fe(f, br, i))),
            pl.BlockSpec((1, 1, d), lambda i, f, be, br: (be[i], 0, 0)),
        ],
        out_specs=pl.BlockSpec((tm, d), lambda i, f, be, br: (i, 0)),
        scratch_shapes=[pltpu.VMEM((tm, d), BF16)],
    )
    return pl.pallas_call(
        functools.partial(_moe_kernel, n_f=n_f),
        out_shape=jax.ShapeDtypeStruct((n_slots, d), F32),
        grid_spec=grid_spec,
        compiler_params=_cparams(("arbitrary", "arbitrary")),
        name="moe",
    )(blk_e, blk_rows, xs, w_gu, w_gu, w_dn,
      b_gu.reshape(n_exp, 1, two_ff), b_gu.reshape(n_exp, 1, two_ff), b_dn.reshape(n_exp, 1, d))


def _route(top_i, n_exp, tt, tm):
    n_tok = top_i.shape[0]
    n_tiles = n_tok // tt
    n_a = tt * TOP_K
    n_blocks = -(-(n_tok * TOP_K + n_tiles * n_exp * (SUBLANES - 1)) // tm) + n_exp
    e = top_i.reshape(n_tiles, n_a)
    onehot = (e[:, :, None] == jnp.arange(n_exp, dtype=jnp.int32)).astype(jnp.int32)
    csum = jnp.cumsum(onehot, axis=1)
    rank = jnp.sum((csum - 1) * onehot, axis=-1)
    cnt = csum[:, -1, :]
    seg = (cnt + SUBLANES - 1) // SUBLANES * SUBLANES
    loff = jnp.cumsum(seg, axis=1) - seg
    lpos = jnp.sum(onehot * loff[:, None, :], axis=-1) + rank
    group = jnp.sum(seg, axis=0)
    group_pad = (group + tm - 1) // tm * tm
    pad_end = jnp.cumsum(group_pad)
    base = pad_end - group_pad
    gbase = base[None, :] + jnp.cumsum(seg, axis=0) - seg
    zero_col = jnp.zeros((n_tiles, 1), jnp.int32)
    tile_meta = jnp.concatenate([loff, zero_col, gbase, zero_col, seg // SUBLANES, zero_col], axis=1)
    fill_meta = jnp.concatenate([
        jnp.zeros((n_exp + 1,), jnp.int32),
        base + group, pad_end[-1:],
        (group_pad - group) // SUBLANES, n_blocks - pad_end[-1:] // tm])
    meta = jnp.concatenate([tile_meta, fill_meta[None]], axis=0).astype(jnp.int32)
    blk_start = jnp.arange(n_blocks, dtype=jnp.int32) * tm
    blk_e = jnp.minimum(jnp.sum(blk_start[:, None] >= pad_end[None, :], axis=1), n_exp - 1).astype(jnp.int32)
    blk_rows = jnp.clip(base[blk_e] + group[blk_e] - blk_start, 0, tm).astype(jnp.int32)
    return (lpos.astype(jnp.int32).reshape(n_tiles, 1, n_a), meta.reshape(n_tiles + 1, 1, 3 * (n_exp + 1)),
            blk_e, blk_rows, n_blocks)


def _post2_kernel(lpos_ref, tg_ref, meta_ref, meta_nxt_ref, ys_hbm, x_ref, g2_ref, lng_ref, lnb_ref, o_ref,
                  stage_scr, sem, *, tt, n_exp, n_tiles, alpha):
    i = pl.program_id(0)
    slot = i % 2

    @pl.when(i == 0)
    def _():
        stage_scr[...] = jnp.zeros(stage_scr.shape, stage_scr.dtype)

    def chunk_copy(s, src_row, dst_row):
        return pltpu.make_async_copy(ys_hbm.at[pl.ds(src_row, SUBLANES), :],
                                     stage_scr.at[s, pl.ds(dst_row, SUBLANES), :], sem.at[s])

    n_seg = n_exp + 1

    def fetch(m_ref, s):
        for e in range(n_exp):
            loff, gbase, n_chunk = m_ref[0, 0, e], m_ref[0, 0, n_seg + e], m_ref[0, 0, 2 * n_seg + e]

            def start(c, _, loff=loff, gbase=gbase):
                chunk_copy(s, pl.multiple_of(gbase + c * SUBLANES, SUBLANES),
                           pl.multiple_of(loff + c * SUBLANES, SUBLANES)).start(priority=e % DMA_PRIORITIES)
                return 0
            lax.fori_loop(0, n_chunk, start, 0)

    @pl.when(i == 0)
    def _():
        fetch(meta_ref, 0)

    @pl.when(i + 1 < n_tiles)
    def _():
        fetch(meta_nxt_ref, 1 - slot)

    total = meta_ref[0, 0, 2 * n_seg]
    for e in range(1, n_exp):
        total = total + meta_ref[0, 0, 2 * n_seg + e]

    def wait(c, _):
        chunk_copy(slot, 0, 0).wait()
        return 0
    lax.fori_loop(0, total, wait, 0)

    lpos = lpos_ref[...]
    tg = tg_ref[...]
    col = lax.broadcasted_iota(jnp.int32, (tt, stage_scr.shape[1]), 1)
    sel = jnp.where(col == lpos[:, 0:1], tg[:, 0:1], 0.0)
    for k in range(1, TOP_K):
        sel = sel + jnp.where(col == lpos[:, k:k + 1], tg[:, k:k + 1], 0.0)
    y = jnp.dot(sel.astype(BF16), stage_scr[slot].astype(BF16), preferred_element_type=F32)
    o_ref[...] = _layernorm(alpha * x_ref[...] + g2_ref[0] * y, lng_ref[...], lnb_ref[...])


def _post2(ys, lpos, top_g, meta, x1, mod, ln_g, ln_b, tok_per_mod, alpha, tile_off, tt, n_exp):
    n_tok, d = x1.shape
    n_tiles = n_tok // tt
    last_tile = meta.shape[0] - 2
    stage_rows = -(-(tt * TOP_K + n_exp * SUBLANES) // LANES) * LANES
    cur = lambda i: (tile_off + i, 0, 0)
    nxt = lambda i: (jnp.minimum(tile_off + i + 1, last_tile), 0, 0)
    smem = lambda a, fn: pl.BlockSpec((1, 1, a.shape[-1]), fn, memory_space=pltpu.SMEM)
    row = lambda a: pl.BlockSpec((1, a.shape[-1]), lambda i: (0, 0))
    return pl.pallas_call(
        functools.partial(_post2_kernel, tt=tt, n_exp=n_exp, n_tiles=n_tiles, alpha=alpha),
        out_shape=jax.ShapeDtypeStruct((n_tok, d), F32),
        grid=(n_tiles,),
        in_specs=[
            pl.BlockSpec((tt, TOP_K), lambda i: (tile_off + i, 0)),
            pl.BlockSpec((tt, LANES), lambda i: (i, 0)),
            smem(meta, cur), smem(meta, nxt),
            pl.BlockSpec(memory_space=pl.ANY),
            pl.BlockSpec((tt, d), lambda i: (i, 0)),
            pl.BlockSpec((1, 1, d), lambda i: ((i * tt) // tok_per_mod, 0, 5)),
            row(ln_g), row(ln_b),
        ],
        out_specs=pl.BlockSpec((tt, d), lambda i: (i, 0)),
        scratch_shapes=[
            pltpu.VMEM((2, stage_rows, d), F32),
            pltpu.SemaphoreType.DMA((2,)),
        ],
        compiler_params=_cparams(("arbitrary",)),
        name="post2",
    )(lpos, top_g, meta, meta, ys, x1, mod, ln_g, ln_b)


def _router_weights(w_router, b_router):
    d, n_exp = w_router.shape
    hi = w_router.astype(BF16)
    lo = (w_router - hi.astype(F32)).astype(BF16)
    pad = lambda a: jnp.pad(a, ((0, 0), (0, LANES - n_exp)))
    return jnp.concatenate([pad(hi), pad(lo)], axis=1), jnp.pad(b_router, (0, LANES - n_exp)).reshape(1, LANES)


def kernel(x_prompt, x_sample, cache_k, cache_v, state_h_fwd, state_h_bwd, c, c_ctx, w_ada, b_ada, w_in, b_in,
           conv_w, conv_b, rg_wa, rg_ba, rg_wx, rg_bx, rg_lam, rpb, w_rnn_out, w_na_out, w_o, b_o,
           ln1_g, ln1_b, ln2_g, ln2_b, w_router, b_router, w_gu, b_gu, w_dn, b_dn):
    depth, d, _ = w_in.shape
    bsz, seq, _ = x_prompt.shape
    dbsz, dseq, _ = x_sample.shape
    n_heads, hd = cache_k.shape[2], cache_k.shape[4]
    d_na = n_heads * hd
    d_rnn = conv_w.shape[2]
    n_exp = w_router.shape[2]
    alpha = (2 * depth) ** 0.25
    n_p, n_s = bsz * seq, dbsz * dseq

    xp = x_prompt.reshape(n_p, d)
    xs = x_sample.reshape(n_s, d)
    n_cond = -(-(1 + dbsz) // SUBLANES) * SUBLANES
    cond = jnp.concatenate([c_ctx[None], c, jnp.zeros((n_cond - 1 - dbsz, d), F32)], axis=0)
    ks, vs, hfs, hbs = [], [], [], []
    for l in range(depth):
        mod = _ada(cond, w_ada[l], b_ada[l]).reshape(n_cond, 1, 6 * d)
        mod_p, mod_s = mod[0:1], mod[1:1 + dbsz]
        w_in_l = w_in[l].astype(BF16)
        w_gates = jnp.concatenate([rg_wa[l], rg_wx[l]], axis=-1).astype(BF16)
        w_rnn_l, w_na_l, w_o_l = w_rnn_out[l].astype(BF16), w_na_out[l].astype(BF16), w_o[l].astype(BF16)
        w_rt, b_rt = _router_weights(w_router[l], b_router[l])
        row = lambda a: a[l].reshape(1, -1)

        def mixer(x, mod_g, n_b, t, h0, tok_per_mod, is_ctx):
            xr, qkv, g = _inproj(x, mod_g, w_in_l, b_in[l], d_rnn, d_na, tok_per_mod, F32 if is_ctx else BF16)
            yf, yb, h_last = _rglru(xr.reshape(n_b, t, d_rnn), conv_w[l], conv_b[l], w_gates,
                                    rg_ba[l], rg_bx[l], rg_lam[l], h0, is_ctx)
            qkv3 = qkv.reshape(n_b, t, 3 * d_na)
            if is_ctx:
                ya = _cattn(qkv3, n_heads, hd)
            else:
                ya = _nattn(qkv3, cache_k[:, l], cache_v[:, l], rpb[l], n_heads, hd)
            merged = _merge(yf.reshape(n_b * t, d_rnn), yb.reshape(n_b * t, d_rnn), ya.reshape(n_b * t, d_na),
                            g, w_rnn_l, w_na_l)
            x1, u2, ti, tg = _post1(merged, x, mod_g, w_o_l, row(b_o), row(ln1_g), row(ln1_b), w_rt, b_rt,
                                    tok_per_mod, alpha, n_exp)
            return x1, u2, ti, tg, qkv3, h_last

        x1p, u2p, tip, tgp, qkv_p, hl_p = mixer(xp, mod_p, bsz, seq, jnp.zeros((bsz, 2, d_rnn), F32), n_p, True)
        h0_s = jnp.stack([state_h_fwd[:, l], state_h_bwd[:, l]], axis=1)
        x1s, u2s, tis, tgs, _, _ = mixer(xs, mod_s, dbsz, dseq, h0_s, dseq, False)

        n_all = n_p + n_s
        tt = _pick(np.gcd(n_p, dseq), MOE_TT)
        tm = MOE_TM
        top_i = jnp.concatenate([tip[:, :TOP_K], tis[:, :TOP_K]], axis=0)
        lpos, meta, blk_e, blk_rows, n_blocks = _route(top_i, n_exp, tt, tm)
        xsort = _dispatch(u2p, u2s, lpos, meta, n_blocks * tm, tt, tm, n_exp)
        ys = _moe(xsort, blk_e, blk_rows, w_gu[l], b_gu[l], w_dn[l], b_dn[l], tm)
        lpos_tok = lpos.reshape(n_all, TOP_K)
        xp = _post2(ys, lpos_tok, tgp, meta, x1p, mod_p, row(ln2_g), row(ln2_b), n_p, alpha, 0, tt, n_exp)
        xs = _post2(ys, lpos_tok, tgs, meta, x1s, mod_s, row(ln2_g), row(ln2_b), dseq, alpha, n_p // tt, tt, n_exp)

        heads = lambda a: a.reshape(bsz, seq, n_heads, hd).transpose(0, 2, 1, 3)
        ks.append(heads(qkv_p[:, :, d_na:2 * d_na]))
        vs.append(heads(qkv_p[:, :, 2 * d_na:]))
        hfs.append(hl_p[:, 0])
        hbs.append(hl_p[:, 1])

    return (xp.reshape(bsz, seq, d), xs.reshape(dbsz, dseq, d),
            jnp.stack(ks, axis=1), jnp.stack(vs, axis=1), jnp.stack(hfs, axis=1), jnp.stack(hbs, axis=1))
```
